```python
import math
import jax
import jax.numpy as jnp
from jax import lax
import numpy as np


D_MODEL = 2048
BATCH = 2
SEQ = 8192
DEPTH = 4

GRID_W = 64
CTX_LEN = 256
QBLK = 128
ROPE_THETA = 10000.0
EPS = 1e-6
N_MOD = 6

N_BRANCH = 4
BR_W = D_MODEL // N_BRANCH

DA_HEADS = 4
DA_DK = BR_W // (2 * DA_HEADS)
DA_DV = 2 * DA_DK

GQ_HEADS = 4
GQ_KV = 2
GQ_HD = BR_W // GQ_HEADS

SC_K = 3
CF_K = 31

PK_HEADS = 8
PK_DQ = 256
N_KEYS = 128
PK_TOPK = 16
N_EXPERTS = N_KEYS * N_KEYS
PEER_CHUNK = 128

IN_WIDTHS = (DA_HEADS * 2 * DA_DK, DA_HEADS * 2 * DA_DK, DA_HEADS * DA_DV,
             GQ_HEADS * GQ_HD, GQ_KV * GQ_HD, GQ_KV * GQ_HD,
             BR_W, BR_W, BR_W,
             2 * BR_W,
             N_BRANCH * D_MODEL)
IN_COLS = sum(IN_WIDTHS)
SPLIT_AT = tuple(int(v) for v in np.cumsum(IN_WIDTHS)[:-1])

kernel_name = 'hybrid_gated_mixers_peer_dit'


def rms_norm(x, g):
    xf = x.astype(jnp.float32)
    y = xf * lax.rsqrt(jnp.mean(xf * xf, axis=-1, keepdims=True) + EPS)
    return (y * g).astype(x.dtype)


def layer_norm(x, g, b):
    xf = x.astype(jnp.float32)
    mu = jnp.mean(xf, axis=-1, keepdims=True)
    var = jnp.mean(jnp.square(xf - mu), axis=-1, keepdims=True)
    return ((xf - mu) * lax.rsqrt(var + EPS) * g + b).astype(x.dtype)


def modulate(h, shift, scale):
    return h * (1.0 + scale) + shift


def rope_tables(row, col, dim):
    n_pair = dim // 4
    inv = ROPE_THETA ** (-jnp.arange(n_pair, dtype=jnp.float32) / n_pair)
    ang = jnp.concatenate([row[:, None] * inv, col[:, None] * inv], axis=-1)
    return jnp.cos(ang), jnp.sin(ang)


def apply_rope(x, cos, sin):
    shp = (1, cos.shape[0]) + (1,) * (x.ndim - 3) + (cos.shape[-1],)
    c, s = cos.reshape(shp), sin.reshape(shp)
    xp = x.reshape(*x.shape[:-1], -1, 2)
    x1, x2 = xp[..., 0], xp[..., 1]
    out = jnp.stack([x1 * c - x2 * s, x1 * s + x2 * c], axis=-1)
    return out.reshape(x.shape).astype(x.dtype)


def sweep_blocks(fn, q):
    B, S = q.shape[:2]
    nb = S // QBLK
    qb = jnp.swapaxes(q.reshape(B, nb, QBLK, *q.shape[2:]), 0, 1)
    out = jnp.swapaxes(lax.map(fn, qb), 0, 1)
    return out.reshape(B, S, *out.shape[3:])


def depthwise_conv(x, w):
    K = w.shape[0]
    return lax.conv_general_dilated(
        x, w[:, None, :], window_strides=(1,), padding=((K // 2, K // 2),),
        dimension_numbers=('NWC', 'WIO', 'NWC'), feature_group_count=x.shape[-1])


def diff_attention(q, k, v, lam):
    s = jnp.einsum('bqhcd,bkhcd->bchqk', q, k).astype(jnp.float32) * (DA_DK ** -0.5)
    p = jax.nn.softmax(s, axis=-1)
    a = p[:, 0] - lam * p[:, 1]
    return jnp.einsum('bhqk,bkhd->bqhd', a.astype(v.dtype), v)


def diff_mixer(q_l, k_l, v_l, q_c, k_c, v_c, cos, sin, qn_g, kn_g, lam_p, head_g, lam_init, with_ctx):
    def split_heads(q, k, v):
        B, T = q.shape[:2]
        q = rms_norm(q.reshape(B, T, DA_HEADS, 2, DA_DK), qn_g)
        k = rms_norm(k.reshape(B, T, DA_HEADS, 2, DA_DK), kn_g)
        return q, k, v.reshape(B, T, DA_HEADS, DA_DV)
    ql, kl, vl = split_heads(q_l, k_l, v_l)
    qc, kc, vc = split_heads(q_c, k_c, v_c)
    ql, kl = apply_rope(ql, cos, sin), apply_rope(kl, cos, sin)
    lp = lam_p.astype(jnp.float32)
    lam = jnp.exp(jnp.sum(lp[0] * lp[1])) - jnp.exp(jnp.sum(lp[2] * lp[3])) + lam_init
    k_all = jnp.concatenate([kl, kc], axis=1)
    v_all = jnp.concatenate([vl, vc], axis=1)

    def finish(o):
        o = rms_norm(o, head_g) * (1.0 - lam_init)
        return o.reshape(*o.shape[:2], DA_HEADS * DA_DV)
    y_l = finish(sweep_blocks(lambda qb: diff_attention(qb, k_all, v_all, lam), ql))
    y_c = finish(diff_attention(qc, kc, vc, lam)) if with_ctx else None
    return y_l, y_c


def gqa_attention(q, k, v):
    s = jnp.einsum('bqgrd,bkgd->bgrqk', q, k).astype(jnp.float32) * (GQ_HD ** -0.5)
    p = jax.nn.softmax(s, axis=-1)
    return jnp.einsum('bgrqk,bkgd->bqgrd', p.astype(v.dtype), v)


def gqa_mixer(q_l, k_l, v_l, q_c, k_c, v_c, cos, sin, qn_g, kn_g, with_ctx):
    def split_heads(q, k, v):
        B, T = q.shape[:2]
        q = rms_norm(q.reshape(B, T, GQ_HEADS, GQ_HD), qn_g)
        k = rms_norm(k.reshape(B, T, GQ_KV, GQ_HD), kn_g)
        return q, k, v.reshape(B, T, GQ_KV, GQ_HD)
    ql, kl, vl = split_heads(q_l, k_l, v_l)
    qc, kc, vc = split_heads(q_c, k_c, v_c)
    ql, kl = apply_rope(ql, cos, sin), apply_rope(kl, cos, sin)
    group = lambda q: q.reshape(*q.shape[:2], GQ_KV, GQ_HEADS // GQ_KV, GQ_HD)
    flat = lambda o: o.reshape(*o.shape[:2], GQ_HEADS * GQ_HD)
    k_all = jnp.concatenate([kl, kc], axis=1)
    v_all = jnp.concatenate([vl, vc], axis=1)
    y_l = flat(sweep_blocks(lambda qb: gqa_attention(qb, k_all, v_all), group(ql)))
    y_c = flat(gqa_attention(group(qc), kc, vc)) if with_ctx else None
    return y_l, y_c


def short_conv_mixer(b_gate, c_gate, xin, w):
    return b_gate * depthwise_conv(c_gate * xin, w)


def conformer_conv_mixer(glu_in, w, b, ln_g, ln_b):
    a, g = jnp.split(glu_in, 2, axis=-1)
    u = depthwise_conv(a * jax.nn.sigmoid(g), w) + b
    return jax.nn.silu(layer_norm(u, ln_g, ln_b))


def merge_branches(ys, gate_logits, w_br, w_o):
    B, T = gate_logits.shape[:2]
    g = gate_logits.reshape(B, T, N_BRANCH, D_MODEL)
    merged = sum(jax.nn.sigmoid(g[:, :, i]) * (ys[i] @ w_br[i]) for i in range(N_BRANCH))
    return merged @ w_o


def peer_ffn(h, wq, sub_k1, sub_k2, u, v):
    B, T, D = h.shape
    hb = h.reshape(-1, PEER_CHUNK, D)

    def chunk(hc):
        n = hc.shape[0]
        q = (hc @ wq).reshape(n, PK_HEADS, 2, PK_DQ // 2)
        s1 = jnp.einsum('nhd,kd->nhk', q[:, :, 0], sub_k1).astype(jnp.float32)
        s2 = jnp.einsum('nhd,kd->nhk', q[:, :, 1], sub_k2).astype(jnp.float32)
        t1, i1 = lax.top_k(s1, PK_TOPK)
        t2, i2 = lax.top_k(s2, PK_TOPK)
        cand = (t1[..., :, None] + t2[..., None, :]).reshape(n, PK_HEADS, PK_TOPK * PK_TOPK)
        cidx = (i1[..., :, None] * N_KEYS + i2[..., None, :]).reshape(n, PK_HEADS, PK_TOPK * PK_TOPK)
        ts, pos = lax.top_k(cand, PK_TOPK)
        eidx = jnp.take_along_axis(cidx, pos, axis=-1)
        gate = jax.nn.softmax(ts, axis=-1)
        act = jax.nn.gelu(jnp.einsum('nd,nhkd->nhk', hc, u[eidx]).astype(jnp.float32))
        return jnp.einsum('nhk,nhkd->nd', (gate * act).astype(hc.dtype), v[eidx])
    return lax.map(chunk, hb).reshape(B, T, D)


def setup_inputs(seed: int = 0) -> dict:
    key = jax.random.key(seed)
    ks = iter(jax.random.split(key, 32))
    L, D = DEPTH, D_MODEL
    nrm = lambda shape, s: jax.random.normal(next(ks), shape, jnp.float32) * s
    gain = lambda shape: 1.0 + nrm(shape, 0.05)
    return {
        'x': nrm((BATCH, SEQ, D), 1.0),
        'c': nrm((BATCH, D), 1.0),
        'ctx': nrm((BATCH, CTX_LEN, D), 1.0),
        'c_ctx': nrm((D,), 1.0),
        'mod_w': nrm((L, D, N_MOD * D), 0.5 * D ** -0.5),
        'mod_b': nrm((L, N_MOD * D), 0.02),
        'norm1_g': gain((L, D)),
        'norm2_g': gain((L, D)),
        'w_in': nrm((L, D, IN_COLS), D ** -0.5),
        'da_qn_g': gain((L, DA_DK)),
        'da_kn_g': gain((L, DA_DK)),
        'da_lam': nrm((L, 4, DA_DK), 0.1),
        'da_head_g': gain((L, DA_DV)),
        'gq_qn_g': gain((L, GQ_HD)),
        'gq_kn_g': gain((L, GQ_HD)),
        'sc_w': nrm((L, SC_K, BR_W), SC_K ** -0.5),
        'cf_w': nrm((L, CF_K, BR_W), CF_K ** -0.5),
        'cf_b': nrm((L, BR_W), 0.02),
        'cf_ln_g': gain((L, BR_W)),
        'cf_ln_b': nrm((L, BR_W), 0.02),
        'w_branch': nrm((L, N_BRANCH, BR_W, D), BR_W ** -0.5),
        'w_out': nrm((L, D, D), D ** -0.5),
        'pk_wq': nrm((L, D, PK_HEADS * PK_DQ), D ** -0.5),
        'pk_k1': nrm((L, N_KEYS, PK_DQ // 2), (PK_DQ // 2) ** -0.5),
        'pk_k2': nrm((L, N_KEYS, PK_DQ // 2), (PK_DQ // 2) ** -0.5),
        'pk_u': nrm((L, N_EXPERTS, D), D ** -0.5),
        'pk_v': nrm((L, N_EXPERTS, D), 0.5),
    }


def reference(x, c, ctx, c_ctx, mod_w, mod_b, norm1_g, norm2_g, w_in, da_qn_g, da_kn_g, da_lam, da_head_g,
              gq_qn_g, gq_kn_g, sc_w, cf_w, cf_b, cf_ln_g, cf_ln_b, w_branch, w_out,
              pk_wq, pk_k1, pk_k2, pk_u, pk_v):
    B, S, _ = x.shape
    ROWS = S // GRID_W
    row = jnp.repeat(jnp.arange(ROWS, dtype=jnp.float32), GRID_W)
    col = jnp.tile(jnp.arange(GRID_W, dtype=jnp.float32), ROWS)
    cos_a, sin_a = rope_tables(row, col, DA_DK)
    cos_b, sin_b = rope_tables(row, col, GQ_HD)
    sc = jax.nn.silu(c)
    scc = jax.nn.silu(c_ctx)
    xc = ctx
    for l in range(DEPTH):
        with_ctx = l < DEPTH - 1
        lam_init = 0.8 - 0.6 * math.exp(-0.3 * l)
        mod = (sc @ mod_w[l] + mod_b[l]).reshape(B, 1, N_MOD, D_MODEL)
        mod_c = (scc @ mod_w[l] + mod_b[l]).reshape(1, 1, N_MOD, D_MODEL)

        h = modulate(rms_norm(x, norm1_g[l]), mod[:, :, 0], mod[:, :, 1])
        hc = modulate(rms_norm(xc, norm1_g[l]), mod_c[:, :, 0], mod_c[:, :, 1])
        pl = jnp.split(h @ w_in[l], SPLIT_AT, axis=-1)
        pc = jnp.split(hc @ w_in[l], SPLIT_AT, axis=-1)
        ya_l, ya_c = diff_mixer(pl[0], pl[1], pl[2], pc[0], pc[1], pc[2], cos_a, sin_a,
                                da_qn_g[l], da_kn_g[l], da_lam[l], da_head_g[l], lam_init, with_ctx)
        yb_l, yb_c = gqa_mixer(pl[3], pl[4], pl[5], pc[3], pc[4], pc[5], cos_b, sin_b,
                               gq_qn_g[l], gq_kn_g[l], with_ctx)
        ys_l = (ya_l, yb_l,
                short_conv_mixer(pl[6], pl[7], pl[8], sc_w[l]),
                conformer_conv_mixer(pl[9], cf_w[l], cf_b[l], cf_ln_g[l], cf_ln_b[l]))
        x = x + mod[:, :, 2] * merge_branches(ys_l, pl[10], w_branch[l], w_out[l])

        h2 = modulate(rms_norm(x, norm2_g[l]), mod[:, :, 3], mod[:, :, 4])
        x = x + mod[:, :, 5] * peer_ffn(h2, pk_wq[l], pk_k1[l], pk_k2[l], pk_u[l], pk_v[l])

        if with_ctx:
            ys_c = (ya_c, yb_c,
                    short_conv_mixer(pc[6], pc[7], pc[8], sc_w[l]),
                    conformer_conv_mixer(pc[9], cf_w[l], cf_b[l], cf_ln_g[l], cf_ln_b[l]))
            xc = xc + mod_c[:, :, 2] * merge_branches(ys_c, pc[10], w_branch[l], w_out[l])
            h2c = modulate(rms_norm(xc, norm2_g[l]), mod_c[:, :, 3], mod_c[:, :, 4])
            xc = xc + mod_c[:, :, 5] * peer_ffn(h2c, pk_wq[l], pk_k1[l], pk_k2[l], pk_u[l], pk_v[l])
    return x
```

```python
import functools
import math

import numpy as np
import jax
import jax.numpy as jnp
from jax import lax
from jax.experimental import pallas as pl
from jax.experimental.pallas import tpu as pltpu

F32 = jnp.float32
BF16 = jnp.bfloat16

EPS = 1e-6
N_MOD = 6
GRID_W = 64
ROPE_THETA = 10000.0
N_BRANCH = 4
DA_HEADS = 4
GQ_HEADS = 4
GQ_KV = 2
SC_K = 3
CF_K = 31
PK_HEADS = 8
N_KEYS = 128
PK_TOPK = 16

LANES = 128
V7X_VMEM_BYTES = 64 * 1024 * 1024
VMEM_CAP = V7X_VMEM_BYTES - 8 * 1024 * 1024

ROW_TILE = 512
CONV_TILE = 256
CONV_HALO = 16
ATT_TQ = 256
ATT_TK = 2048
PEER_TE = 512

QK_COLS = (2 * DA_HEADS + GQ_HEADS + GQ_KV) * LANES
V_DA_BLK = QK_COLS // LANES
V_GQ_BLK = V_DA_BLK + DA_HEADS
CONV_COL0 = (V_GQ_BLK + GQ_KV) * LANES
CONV_COLS = 5 * 512
P_COLS = CONV_COL0 + CONV_COLS


def _cparams(sem, vmem_bytes):
    limit = int(min(max(vmem_bytes * 5 // 4 + (4 << 20), 16 << 20), VMEM_CAP))
    return pltpu.CompilerParams(dimension_semantics=sem, vmem_limit_bytes=limit)


def _group_of(tile, tile_rows, n_lat_rows, seq, n_batch):
    n_lat_tiles = n_lat_rows // tile_rows
    return jnp.where(tile >= n_lat_tiles, n_batch, tile // (seq // tile_rows))


def _mod_kernel(c_ref, w_ref, b_ref, o_ref):
    c = c_ref[...]
    sc = c * (1.0 / (1.0 + jnp.exp(-c)))
    o_ref[0] = jnp.dot(sc.astype(BF16), w_ref[0].astype(BF16), preferred_element_type=F32) + b_ref[0]


def _modulation(cs, mod_w, mod_b):
    L, D, W = mod_w.shape
    tn = 1024
    return pl.pallas_call(
        _mod_kernel,
        grid=(L, W // tn),
        in_specs=[pl.BlockSpec((8, D), lambda l, j: (0, 0)),
                  pl.BlockSpec((1, D, tn), lambda l, j: (l, 0, j)),
                  pl.BlockSpec((1, 1, tn), lambda l, j: (l, 0, j))],
        out_specs=pl.BlockSpec((1, 8, tn), lambda l, j: (l, 0, j)),
        out_shape=jax.ShapeDtypeStruct((L, 8, W), F32),
        compiler_params=_cparams(("parallel", "parallel"), 2 * D * tn * 4 + D * tn * 2),
        name="modulation",
    )(cs, mod_w, mod_b.reshape(L, 1, W))


def _nmm_kernel(x_ref, g_ref, mod_ref, w_ref, *rest, shift, scale, emit_h):
    if emit_h:
        o_ref, h_ref, h_scr = rest
    else:
        o_ref, h_scr = rest

    @pl.when(pl.program_id(1) == 0)
    def _():
        x = x_ref[...]
        ms = jnp.mean(x * x, axis=-1, keepdims=True)
        y = x * lax.rsqrt(ms + EPS) * g_ref[...]
        h = y * (1.0 + mod_ref[0, scale:scale + 1, :]) + mod_ref[0, shift:shift + 1, :]
        h_scr[...] = h.astype(BF16)
        if emit_h:
            h_ref[...] = h_scr[...]

    o_ref[...] = jnp.dot(h_scr[...], w_ref[...], preferred_element_type=F32).astype(o_ref.dtype)


def _norm_mod_matmul(x, g, mod_l, w, *, shift, scale, n_rows, n_lat_rows, seq, emit_h, name):
    D = x.shape[1]
    cols = w.shape[1]
    nb = mod_l.shape[0] - 1
    tm, tn = ROW_TILE, (1024 if cols % 1024 == 0 else 512)
    grp = functools.partial(_group_of, tile_rows=tm, n_lat_rows=n_lat_rows, seq=seq, n_batch=nb)
    out_shape = [jax.ShapeDtypeStruct((n_rows, cols), BF16)]
    out_specs = [pl.BlockSpec((tm, tn), lambda i, j: (i, j))]
    if emit_h:
        out_shape.append(jax.ShapeDtypeStruct((n_rows, D), BF16))
        out_specs.append(pl.BlockSpec((tm, D), lambda i, j: (i, 0)))
    vmem = 2 * tm * D * 4 + 2 * D * tn * 2 + 2 * tm * tn * 2 + 3 * tm * D * 2 + 2 * tm * D * 4
    res = pl.pallas_call(
        functools.partial(_nmm_kernel, shift=shift, scale=scale, emit_h=emit_h),
        grid=(n_rows // tm, cols // tn),
        in_specs=[pl.BlockSpec((tm, D), lambda i, j: (i, 0)),
                  pl.BlockSpec((1, D), lambda i, j: (0, 0)),
                  pl.BlockSpec((1, N_MOD, D), lambda i, j: (grp(i), 0, 0)),
                  pl.BlockSpec((D, tn), lambda i, j: (0, j))],
        out_specs=out_specs,
        out_shape=out_shape,
        scratch_shapes=[pltpu.VMEM((tm, D), BF16)],
        compiler_params=_cparams(("parallel", "arbitrary"), vmem),
        name=name,
    )(x, g.reshape(1, D), mod_l, w)
    return res if emit_h else res[0]


def _da_lane_perm():
    j = np.arange(LANES)
    comp, parity, pair = (j // 32) % 2, j // 64, j % 32
    return comp * 64 + 2 * pair + parity


def _gq_lane_perm():
    j = np.arange(LANES)
    return 2 * (j % 64) + j // 64


def _qk_prep_kernel(p_ref, tab_ref, gain_ref, o_ref):
    lane = lax.broadcasted_iota(jnp.int32, (1, LANES), 1)
    comp0 = (lane % 64) < 32
    n_da = 2 * DA_HEADS
    for hd in range(QK_COLS // LANES):
        sl = slice(hd * LANES, (hd + 1) * LANES)
        x = p_ref[:, sl].astype(F32)
        xx = x * x
        tot = jnp.sum(xx, axis=-1, keepdims=True)
        if hd < n_da:
            s0 = jnp.sum(jnp.where(comp0, xx, 0.0), axis=-1, keepdims=True)
            ms = jnp.where(comp0, s0, tot - s0) * (1.0 / 64.0)
            cos, sin = tab_ref[0], tab_ref[1]
        else:
            ms = tot * (1.0 / 128.0)
            cos, sin = tab_ref[2], tab_ref[3]
        y = x * lax.rsqrt(ms + EPS) * gain_ref[hd:hd + 1, :]
        o_ref[:, sl] = (y * cos + pltpu.roll(y, 64, 1) * sin).astype(BF16)


def _qk_prep(p, rope_tab, gains, *, n_lat_rows, seq):
    n = p.shape[0]
    tm = ROW_TILE
    n_lat_tiles = n_lat_rows // tm
    rope_blk = lambda i: jnp.where(i >= n_lat_tiles, seq // tm, i % (seq // tm))
    return pl.pallas_call(
        _qk_prep_kernel,
        grid=(n // tm,),
        in_specs=[pl.BlockSpec((tm, QK_COLS), lambda i: (i, 0)),
                  pl.BlockSpec((4, tm, LANES), lambda i: (0, rope_blk(i), 0)),
                  pl.BlockSpec((QK_COLS // LANES, LANES), lambda i: (0, 0))],
        out_specs=pl.BlockSpec((tm, QK_COLS), lambda i: (i, 0)),
        out_shape=jax.ShapeDtypeStruct((n, QK_COLS), BF16),
        compiler_params=_cparams(("parallel",), 4 * tm * QK_COLS * 2 + 8 * tm * LANES * 4 + 6 * tm * LANES * 4),
        name="qk_prep",
    )(p, rope_tab, gains)


def _flash_kernel(*refs, diff, n_lat_chunks, nq_lat, tk, has_lat):
    if has_lat:
        q_ref, kl_ref, vl_ref, kc_ref, vc_ref, aux_ref, o_ref, m_scr, l_scr, acc_scr = refs
    else:
        q_ref, kc_ref, vc_ref, aux_ref, o_ref, m_scr, l_scr, acc_scr = refs
    tq = q_ref.shape[0]
    if diff:
        q = q_ref[...]
        lane = lax.broadcasted_iota(jnp.int32, (1, LANES), 1)
        comp0 = (lane % 64) < 32
        zero = jnp.zeros_like(q)
        q2 = jnp.concatenate([jnp.where(comp0, q, zero), jnp.where(comp0, zero, q)], axis=0)
    else:
        q2 = jnp.concatenate([q_ref[:, :LANES], q_ref[:, LANES:]], axis=0)

    m_scr[...] = jnp.full(m_scr.shape, -jnp.inf, F32)
    l_scr[...] = jnp.zeros(l_scr.shape, F32)
    acc_scr[...] = jnp.zeros(acc_scr.shape, F32)

    def step(k, v):
        s = lax.dot_general(q2, k, (((1,), (1,)), ((), ())), preferred_element_type=F32)
        m_prev = m_scr[...]
        m_new = jnp.maximum(m_prev, jnp.max(s, axis=-1, keepdims=True))
        alpha = jnp.exp(m_prev - m_new)
        p = jnp.exp(s - m_new)
        l_scr[...] = alpha * l_scr[...] + jnp.sum(p, axis=-1, keepdims=True)
        acc_scr[...] = alpha * acc_scr[...] + jnp.dot(p.astype(BF16), v, preferred_element_type=F32)
        m_scr[...] = m_new

    if has_lat:
        trips = jnp.where(pl.program_id(2) >= nq_lat, 0, n_lat_chunks)

        def body(c, carry):
            off = pl.multiple_of(c * tk, tk)
            step(kl_ref[pl.ds(off, tk), :], vl_ref[pl.ds(off, tk), :])
            return carry

        lax.fori_loop(0, trips, body, 0)
    step(kc_ref[...], vc_ref[...])

    o = acc_scr[...] / l_scr[...]
    if diff:
        d = o[:tq] - aux_ref[0:1, :] * o[tq:]
        ms = jnp.mean(d * d, axis=-1, keepdims=True)
        o_ref[...] = (d * lax.rsqrt(ms + EPS) * aux_ref[1:2, :]).astype(o_ref.dtype)
    else:
        o_ref[:, :LANES] = o[:tq].astype(o_ref.dtype)
        o_ref[:, LANES:] = o[tq:].astype(o_ref.dtype)


def _attention(qk, p, aux, *, diff, n_batch, seq, ctx_len, with_ctx_queries):
    tq, tk = ATT_TQ, min(ATT_TK, seq)
    nq_lat, nq_ctx = seq // tq, ctx_len // tq
    n_lat_rows = n_batch * seq
    if diff:
        n_heads, qw = DA_HEADS, LANES
        q_blk = lambda h: h
        k_blk = lambda h: DA_HEADS + h
        v_blk = lambda h: V_DA_BLK + h
    else:
        n_heads, qw = GQ_KV, 2 * LANES
        q_blk = lambda h: (2 * DA_HEADS * LANES) // qw + h
        k_blk = lambda h: 2 * DA_HEADS + GQ_HEADS + h
        v_blk = lambda h: V_GQ_BLK + h
    nq = nq_lat + (nq_ctx if with_ctx_queries else 0)
    q_row = lambda b, qi: jnp.where(qi >= nq_lat, n_batch * nq_lat + b * nq_ctx + qi - nq_lat, b * nq_lat + qi)
    ctx_row = lambda b: n_lat_rows // ctx_len + b
    kernel = functools.partial(_flash_kernel, diff=diff, n_lat_chunks=seq // tk, nq_lat=nq_lat, tk=tk, has_lat=True)
    n_out = n_lat_rows + (n_batch * ctx_len if with_ctx_queries else 0)
    vmem = 4 * seq * LANES * 2 + 2 * tq * tk * 4 * 3 + 8 * tq * LANES * 4 * 3
    return pl.pallas_call(
        kernel,
        grid=(n_batch, n_heads, nq),
        in_specs=[pl.BlockSpec((tq, qw), lambda b, h, qi: (q_row(b, qi), q_blk(h))),
                  pl.BlockSpec((seq, LANES), lambda b, h, qi: (b, k_blk(h))),
                  pl.BlockSpec((seq, LANES), lambda b, h, qi: (b, v_blk(h))),
                  pl.BlockSpec((ctx_len, LANES), lambda b, h, qi: (ctx_row(b), k_blk(h))),
                  pl.BlockSpec((ctx_len, LANES), lambda b, h, qi: (ctx_row(b), v_blk(h))),
                  pl.BlockSpec((8, LANES), lambda b, h, qi: (0, 0))],
        out_specs=pl.BlockSpec((tq, qw), lambda b, h, qi: (q_row(b, qi), h)),
        out_shape=jax.ShapeDtypeStruct((n_out, n_heads * qw), BF16),
        scratch_shapes=[pltpu.VMEM((2 * tq, 1), F32), pltpu.VMEM((2 * tq, 1), F32),
                        pltpu.VMEM((2 * tq, LANES), F32)],
        compiler_params=_cparams(("parallel", "parallel", "arbitrary"), vmem),
        name="diff_attention" if diff else "gqa_attention",
    )(qk, qk, p, qk, p, aux)


def _conv_kernel(cur_ref, prev_ref, next_ref, w_ref, o_ref, z_scr, u_scr, *, tiles_lat, tiles_ctx, n_lat_tiles):
    t = pl.program_id(0)
    in_lat = t < n_lat_tiles
    pos = jnp.where(in_lat, t % tiles_lat, (t - n_lat_tiles) % tiles_ctx)
    last = jnp.where(in_lat, tiles_lat, tiles_ctx) - 1
    has_prev = (pos > 0).astype(F32)
    has_next = (pos < last).astype(F32)
    T, H = cur_ref.shape[0], CONV_HALO
    W = 512

    def cols(ref, k):
        return ref[:, k * W:(k + 1) * W].astype(F32)

    def glu(ref):
        return cols(ref, 3) * (1.0 / (1.0 + jnp.exp(-cols(ref, 4))))

    z_scr[0:H, :] = cols(prev_ref, 1) * cols(prev_ref, 2) * has_prev
    z_scr[H:H + T, :] = cols(cur_ref, 1) * cols(cur_ref, 2)
    z_scr[H + T:, :] = cols(next_ref, 1) * cols(next_ref, 2) * has_next
    acc = jnp.zeros((T, W), F32)
    for k in range(SC_K):
        off = H + k - SC_K // 2
        acc = acc + w_ref[k:k + 1, :] * z_scr[off:off + T, :]
    o_ref[:, :W] = (cols(cur_ref, 0) * acc).astype(o_ref.dtype)

    u_scr[0:H, :] = glu(prev_ref) * has_prev
    u_scr[H:H + T, :] = glu(cur_ref)
    u_scr[H + T:, :] = glu(next_ref) * has_next
    acc = jnp.zeros((T, W), F32)
    for k in range(CF_K):
        off = H + k - CF_K // 2
        acc = acc + w_ref[SC_K + k:SC_K + k + 1, :] * u_scr[off:off + T, :]
    r = SC_K + CF_K
    u = acc + w_ref[r:r + 1, :]
    mu = jnp.mean(u, axis=-1, keepdims=True)
    var = jnp.mean(jnp.square(u - mu), axis=-1, keepdims=True)
    y = (u - mu) * lax.rsqrt(var + EPS) * w_ref[r + 1:r + 2, :] + w_ref[r + 2:r + 3, :]
    o_ref[:, W:] = (y * (1.0 / (1.0 + jnp.exp(-y)))).astype(o_ref.dtype)


def _conv_mixers(p, conv_w, *, n_rows, n_lat_rows, seq, ctx_len):
    T, H = CONV_TILE, CONV_HALO
    r = T // H
    n_halo_blocks = p.shape[0] // H
    cblk = CONV_COL0 // CONV_COLS
    kernel = functools.partial(_conv_kernel, tiles_lat=seq // T, tiles_ctx=ctx_len // T, n_lat_tiles=n_lat_rows // T)
    return pl.pallas_call(
        kernel,
        grid=(n_rows // T,),
        in_specs=[pl.BlockSpec((T, CONV_COLS), lambda t: (t, cblk)),
                  pl.BlockSpec((H, CONV_COLS), lambda t: (jnp.maximum(t * r - 1, 0), cblk)),
                  pl.BlockSpec((H, CONV_COLS), lambda t: (jnp.minimum((t + 1) * r, n_halo_blocks - 1), cblk)),
                  pl.BlockSpec(conv_w.shape, lambda t: (0, 0))],
        out_specs=pl.BlockSpec((T, 1024), lambda t: (t, 0)),
        out_shape=jax.ShapeDtypeStruct((n_rows, 1024), BF16),
        scratch_shapes=[pltpu.VMEM((T + 2 * H, 512), F32), pltpu.VMEM((T + 2 * H, 512), F32)],
        compiler_params=_cparams(("parallel",), 4 * T * CONV_COLS * 2 + 16 * T * 512 * 4),
        name="conv_mixers",
    )(p, p, p, conv_w)


def _merge_kernel(ya_ref, yb_ref, yc_ref, g_ref, x_ref, mod_ref, wbr_ref, wo_ref, o_ref):
    D = x_ref.shape[1]
    W = 512
    ys = (ya_ref[...], yb_ref[...], yc_ref[:, :W], yc_ref[:, W:])
    merged = jnp.zeros(x_ref.shape, F32)
    for i in range(N_BRANCH):
        z = jnp.dot(ys[i], wbr_ref[i], preferred_element_type=F32)
        gate = g_ref[:, i * D:(i + 1) * D].astype(F32)
        merged = merged + z * (1.0 / (1.0 + jnp.exp(-gate)))
    out = jnp.dot(merged.astype(BF16), wo_ref[...], preferred_element_type=F32)
    o_ref[...] = x_ref[...] + mod_ref[0, 2:3, :] * out


def _merge(ya, yb, yc, gates, x, mod_l, wbr, wo, *, n_rows, n_lat_rows, seq):
    D = x.shape[1]
    nb = mod_l.shape[0] - 1
    tm = CONV_TILE
    grp = functools.partial(_group_of, tile_rows=tm, n_lat_rows=n_lat_rows, seq=seq, n_batch=nb)
    vmem = (2 * tm * N_BRANCH * D * 2 + 4 * tm * D * 4 + N_BRANCH * 512 * D * 2 + D * D * 2
            + 4 * tm * D * 4 + 2 * tm * 2048 * 2)
    return pl.pallas_call(
        _merge_kernel,
        grid=(n_rows // tm,),
        in_specs=[pl.BlockSpec((tm, 512), lambda i: (i, 0)),
                  pl.BlockSpec((tm, 512), lambda i: (i, 0)),
                  pl.BlockSpec((tm, 1024), lambda i: (i, 0)),
                  pl.BlockSpec((tm, N_BRANCH * D), lambda i: (i, 0)),
                  pl.BlockSpec((tm, D), lambda i: (i, 0)),
                  pl.BlockSpec((1, N_MOD, D), lambda i: (grp(i), 0, 0)),
                  pl.BlockSpec(wbr.shape, lambda i: (0, 0, 0), pipeline_mode=pl.Buffered(1)),
                  pl.BlockSpec(wo.shape, lambda i: (0, 0), pipeline_mode=pl.Buffered(1))],
        out_specs=pl.BlockSpec((tm, D), lambda i: (i, 0)),
        out_shape=jax.ShapeDtypeStruct((n_rows, D), F32),
        compiler_params=_cparams(("parallel",), vmem),
        name="merge_branches",
    )(ya, yb, yc, gates, x, mod_l, wbr, wo)


def _top_values(work, n):
    R = work.shape[0]
    idx = lax.broadcasted_iota(jnp.int32, work.shape, 0).astype(F32)
    tops = []
    for _ in range(n):
        m = jnp.max(work, axis=0, keepdims=True)
        tops.append(m)
        first = jnp.min(jnp.where(work == m, idx, float(R)), axis=0, keepdims=True)
        work = jnp.where(idx == first, -jnp.inf, work)
    return tops


def _stack_rows(rows):
    n, T = len(rows), rows[0].shape[1]
    idx = lax.broadcasted_iota(jnp.int32, (n, T), 0)
    out = jnp.zeros((n, T), F32)
    for r, row in enumerate(rows):
        out = jnp.where(idx == r, row, out)
    return out


def _peer_select_kernel(q_ref, k1_ref, k2_ref, s1_ref, s2_ref, b_ref, c_ref, tau_ref):
    K = PK_TOPK
    nt = (((1,), (1,)), ((), ()))
    s1 = lax.dot_general(k1_ref[...], q_ref[:, :N_KEYS], nt, preferred_element_type=F32)
    s2 = lax.dot_general(k2_ref[...], q_ref[:, N_KEYS:], nt, preferred_element_type=F32)
    t1 = _top_values(s1, K)
    t2 = _top_values(s2, K)
    T1, T2 = _stack_rows(t1), _stack_rows(t2)
    ninf = jnp.full((8, s1.shape[1]), -jnp.inf, F32)
    r8 = lax.broadcasted_iota(jnp.int32, (8, s1.shape[1]), 0)
    slabs = [t1[0] + T2[0:8], t1[0] + T2[8:16], t1[1] + T2[0:8]]
    slabs.append(jnp.where(r8 < K // 3, t1[2] + T2[0:8], ninf))
    slabs.append(jnp.where(r8 < K // 4, t1[3] + T2[0:8], ninf))
    slabs.append(T1[8:16] + t2[0])
    slabs.append(jnp.where(r8 >= 4, T1[0:8] + t2[0], ninf))
    slabs.append(jnp.where(r8 >= 4, T1[0:8] + t2[1], ninf))
    slabs.append(jnp.where(r8 == 4, T1[0:8] + t2[2], ninf))
    cand = jnp.concatenate(slabs, axis=0)
    best = _top_values(cand, K)
    top, tau = best[0], best[K - 1]
    z = jnp.sum(jnp.where(cand >= tau, jnp.exp(cand - top), 0.0), axis=0, keepdims=True)
    s1_ref[0] = s1
    s2_ref[0] = s2
    b_ref[0] = jnp.exp(s1 - t1[0]) / z
    c_ref[0] = jnp.exp(s2 - t2[0])
    tau_ref[0] = tau


def _peer_select(q, k1, k2):
    n = q.shape[0]
    tm = ROW_TILE
    big = jax.ShapeDtypeStruct((PK_HEADS, N_KEYS, n), F32)
    big_spec = pl.BlockSpec((1, N_KEYS, tm), lambda i, h: (h, 0, i))
    return pl.pallas_call(
        _peer_select_kernel,
        grid=(n // tm, PK_HEADS),
        in_specs=[pl.BlockSpec((tm, 2 * N_KEYS), lambda i, h: (i, h)),
                  pl.BlockSpec(k1.shape, lambda i, h: (0, 0)),
                  pl.BlockSpec(k2.shape, lambda i, h: (0, 0))],
        out_specs=[big_spec, big_spec, big_spec, big_spec, pl.BlockSpec((1, 1, tm), lambda i, h: (h, 0, i))],
        out_shape=[big, big, big, big, jax.ShapeDtypeStruct((PK_HEADS, 1, n), F32)],
        compiler_params=_cparams(("parallel", "parallel"), 8 * N_KEYS * tm * 4 + 32 * N_KEYS * tm * 4),
        name="peer_select",
    )(q, k1, k2)


def _gelu_tanh(x):
    return 0.5 * x * (1.0 + jnp.tanh(math.sqrt(2.0 / math.pi) * (x + 0.044715 * (x * x * x))))


def _peer_dense_kernel(h_ref, u_ref, vt_ref, s1r_ref, br_ref, s2_ref, c_ref, tau_ref, x_ref, mod_ref, o_ref,
                       acc_scr, s_scr, w_scr):
    j = pl.program_id(1)

    @pl.when(j == 0)
    def _():
        acc_scr[...] = jnp.zeros(acc_scr.shape, F32)

    s_scr[...] = lax.dot_general(u_ref[...], h_ref[...], (((1,), (1,)), ((), ())), preferred_element_type=F32)
    for b in range(u_ref.shape[0] // N_KEYS):
        rows = slice(b * N_KEYS, (b + 1) * N_KEYS)
        gate = jnp.zeros((N_KEYS, h_ref.shape[0]), F32)
        for hd in range(PK_HEADS):
            total = s2_ref[hd] + s1r_ref[b, hd:hd + 1, :]
            val = c_ref[hd] * br_ref[b, hd:hd + 1, :]
            gate = gate + jnp.where(total >= tau_ref[hd], val, 0.0)
        w_scr[rows, :] = (gate * _gelu_tanh(s_scr[rows, :])).astype(BF16)
    acc_scr[...] += jnp.dot(vt_ref[...], w_scr[...], preferred_element_type=F32)

    @pl.when(j == pl.num_programs(1) - 1)
    def _():
        o_ref[...] = x_ref[...] + mod_ref[0, 5:6, :] * acc_scr[...].T


def _peer_dense(h, u, vt, s1r, br, s2, c, tau, x, mod_l, *, n_rows, n_lat_rows, seq):
    D = x.shape[1]
    n_exp = u.shape[0]
    nb = mod_l.shape[0] - 1
    tm, te = ROW_TILE, PEER_TE
    nk = te // N_KEYS
    grp = functools.partial(_group_of, tile_rows=tm, n_lat_rows=n_lat_rows, seq=seq, n_batch=nb)
    sel_spec = pl.BlockSpec((PK_HEADS, N_KEYS, tm), lambda i, j: (0, 0, i))
    row_spec = pl.BlockSpec((nk, PK_HEADS, tm), lambda i, j: (j, 0, i))
    vmem = (2 * tm * D * 2 + 4 * te * D * 2 + 4 * PK_HEADS * N_KEYS * tm * 4 + 4 * tm * D * 4
            + D * tm * 4 + te * tm * 6 + 2 * D * tm * 4)
    return pl.pallas_call(
        _peer_dense_kernel,
        grid=(n_rows // tm, n_exp // te),
        in_specs=[pl.BlockSpec((tm, D), lambda i, j: (i, 0)),
                  pl.BlockSpec((te, D), lambda i, j: (j, 0)),
                  pl.BlockSpec((D, te), lambda i, j: (0, j)),
                  row_spec, row_spec, sel_spec, sel_spec,
                  pl.BlockSpec((PK_HEADS, 1, tm), lambda i, j: (0, 0, i)),
                  pl.BlockSpec((tm, D), lambda i, j: (i, 0)),
                  pl.BlockSpec((1, N_MOD, D), lambda i, j: (grp(i), 0, 0))],
        out_specs=pl.BlockSpec((tm, D), lambda i, j: (i, 0)),
        out_shape=jax.ShapeDtypeStruct((n_rows, D), F32),
        scratch_shapes=[pltpu.VMEM((D, tm), F32), pltpu.VMEM((te, tm), F32), pltpu.VMEM((te, tm), BF16)],
        compiler_params=_cparams(("parallel", "arbitrary"), vmem),
        name="peer_dense",
    )(h, u, vt, s1r, br, s2, c, tau, x, mod_l)


def _rope_tables(seq, pad_rows):
    t = jnp.arange(seq, dtype=jnp.int32)
    row = (t // GRID_W).astype(F32)
    col = (t % GRID_W).astype(F32)

    def tables(dim, lanes_per_pair):
        n_pair = dim // 4
        inv = ROPE_THETA ** (-jnp.arange(n_pair, dtype=F32) / n_pair)
        ang = jnp.concatenate([row[:, None] * inv, col[:, None] * inv], axis=-1)
        lane_pair = np.arange(LANES) % lanes_per_pair
        sign = np.where(np.arange(LANES) < 64, -1.0, 1.0).astype(np.float32)
        return jnp.cos(ang)[:, lane_pair], jnp.sin(ang)[:, lane_pair] * sign

    ca, sa = tables(64, 32)
    cb, sb = tables(128, 64)
    tab = jnp.stack([ca, sa, cb, sb])
    ident = jnp.stack([jnp.ones((pad_rows, LANES), F32), jnp.zeros((pad_rows, LANES), F32)] * 2)
    return jnp.concatenate([tab, ident], axis=1)


def _proj_col_perm():
    da, gq = _da_lane_perm(), _gq_lane_perm()
    cols = []
    for base in (0, 512):
        cols += [base + h * LANES + da for h in range(DA_HEADS)]
    cols += [1536 + h * LANES + gq for h in range(GQ_HEADS)]
    cols += [2048 + h * LANES + gq for h in range(GQ_KV)]
    cols += [np.arange(1024, 1536), np.arange(2304, 2560), np.arange(2560, 5120)]
    return np.concatenate(cols)


def _qk_gains(da_qn_g, da_kn_g, gq_qn_g, gq_kn_g):
    da, gq = _da_lane_perm() % 64, _gq_lane_perm()
    rows = ([da_qn_g[da] * (64 ** -0.5)] * DA_HEADS + [da_kn_g[da]] * DA_HEADS
            + [gq_qn_g[gq] * (128 ** -0.5)] * GQ_HEADS + [gq_kn_g[gq]] * GQ_KV)
    return jnp.stack(rows).astype(F32)


def kernel(x, c, ctx, c_ctx, mod_w, mod_b, norm1_g, norm2_g, w_in, da_qn_g, da_kn_g, da_lam, da_head_g,
           gq_qn_g, gq_kn_g, sc_w, cf_w, cf_b, cf_ln_g, cf_ln_b, w_branch, w_out,
           pk_wq, pk_k1, pk_k2, pk_u, pk_v):
    B, S, D = x.shape
    C = ctx.shape[1]
    L = mod_w.shape[0]
    n_lat, n_all = B * S, B * S + B * C
    assert S % ROW_TILE == 0 and (B * C) % ROW_TILE == 0 and C % CONV_TILE == 0 and S % min(ATT_TK, S) == 0
    assert B + 1 <= 8 and D == N_BRANCH * 512 and w_in.shape[2] == 5120 + N_BRANCH * D

    xs = jnp.concatenate([x.reshape(n_lat, D), ctx.reshape(B * C, D)], axis=0)
    cs = jnp.zeros((8, D), F32).at[:B].set(c).at[B].set(c_ctx)
    mod = _modulation(cs, mod_w, mod_b).reshape(L, 8, N_MOD, D)[:, :B + 1]
    rope_tab = _rope_tables(S, ROW_TILE)
    perm = _proj_col_perm()
    geom = dict(n_lat_rows=n_lat, seq=S)

    for l in range(L):
        with_ctx = l < L - 1
        n_rows = n_all if with_ctx else n_lat
        lam_init = 0.8 - 0.6 * math.exp(-0.3 * l)
        lp = da_lam[l].astype(F32)
        lam = jnp.exp(jnp.sum(lp[0] * lp[1])) - jnp.exp(jnp.sum(lp[2] * lp[3])) + lam_init
        aux = jnp.zeros((8, LANES), F32).at[0].set(lam).at[1].set(da_head_g[l] * (1.0 - lam_init))
        w_p = w_in[l][:, perm].astype(BF16)
        w_g = w_in[l][:, 5120:].astype(BF16)
        conv_w = jnp.concatenate([sc_w[l], cf_w[l], cf_b[l][None], cf_ln_g[l][None], cf_ln_b[l][None],
                                  jnp.zeros((3, 512), F32)], axis=0)
        mod_l = mod[l]

        p = _norm_mod_matmul(xs, norm1_g[l], mod_l, w_p, shift=0, scale=1, n_rows=n_all, emit_h=False,
                             name="in_proj", **geom)
        gates = _norm_mod_matmul(xs, norm1_g[l], mod_l, w_g, shift=0, scale=1, n_rows=n_rows, emit_h=False,
                                 name="gate_proj", **geom)
        qk = _qk_prep(p, rope_tab, _qk_gains(da_qn_g[l], da_kn_g[l], gq_qn_g[l], gq_kn_g[l]), **geom)
        att = dict(n_batch=B, seq=S, ctx_len=C, with_ctx_queries=with_ctx)
        ya = _attention(qk, p, aux, diff=True, **att)
        yb = _attention(qk, p, aux, diff=False, **att)
        yc = _conv_mixers(p, conv_w, n_rows=n_rows, ctx_len=C, **geom)
        xs = _merge(ya, yb, yc, gates, xs, mod_l, w_branch[l].astype(BF16), w_out[l].astype(BF16),
                    n_rows=n_rows, **geom)

        q, h2 = _norm_mod_matmul(xs, norm2_g[l], mod_l, pk_wq[l].astype(BF16), shift=3, scale=4, n_rows=n_rows,
                                 emit_h=True, name="peer_query", **geom)
        s1, s2, bf, cf, tau = _peer_select(q, pk_k1[l].astype(BF16), pk_k2[l].astype(BF16))
        s1r, br = jnp.transpose(s1, (1, 0, 2)), jnp.transpose(bf, (1, 0, 2))
        xs = _peer_dense(h2, pk_u[l].astype(BF16), pk_v[l].T.astype(BF16), s1r, br, s2, cf, tau, xs, mod_l,
                         n_rows=n_rows, **geom)
    return xs[:n_lat].reshape(B, S, D)
```

```python
import functools
import math

import numpy as np
import jax
import jax.numpy as jnp
from jax import lax
from jax.experimental import pallas as pl
from jax.experimental.pallas import tpu as pltpu

F32 = jnp.float32
BF16 = jnp.bfloat16

EPS = 1e-6
N_MOD = 6
GRID_W = 64
ROPE_THETA = 10000.0
N_BRANCH = 4
DA_HEADS = 4
GQ_HEADS = 4
GQ_KV = 2
SC_K = 3
CF_K = 31
PK_HEADS = 8
N_KEYS = 128
PK_TOPK = 16

LANES = 128
V7X_VMEM_BYTES = 64 * 1024 * 1024
VMEM_CAP = V7X_VMEM_BYTES - 8 * 1024 * 1024

ROW_TILE = 512
CONV_TILE = 256
CONV_HALO = 16
ATT_TQ = 256
ATT_TK = 2048
PEER_TE = 512

QK_COLS = (2 * DA_HEADS + GQ_HEADS + GQ_KV) * LANES
V_DA_BLK = QK_COLS // LANES
V_GQ_BLK = V_DA_BLK + DA_HEADS
CONV_COL0 = (V_GQ_BLK + GQ_KV) * LANES
CONV_COLS = 5 * 512
P_COLS = CONV_COL0 + CONV_COLS


def _cparams(sem, vmem_bytes):
    limit = int(min(max(vmem_bytes * 5 // 4 + (4 << 20), 16 << 20), VMEM_CAP))
    return pltpu.CompilerParams(dimension_semantics=sem, vmem_limit_bytes=limit)


def _group_of(tile, tile_rows, n_lat_rows, seq, n_batch):
    n_lat_tiles = n_lat_rows // tile_rows
    return jnp.where(tile >= n_lat_tiles, n_batch, tile // (seq // tile_rows))


def _mod_kernel(c_ref, w_ref, b_ref, o_ref):
    c = c_ref[...]
    sc = c * (1.0 / (1.0 + jnp.exp(-c)))
    o_ref[0] = jnp.dot(sc.astype(BF16), w_ref[0].astype(BF16), preferred_element_type=F32) + b_ref[0]


def _modulation(cs, mod_w, mod_b):
    L, D, W = mod_w.shape
    tn = 1024
    return pl.pallas_call(
        _mod_kernel,
        grid=(L, W // tn),
        in_specs=[pl.BlockSpec((8, D), lambda l, j: (0, 0)),
                  pl.BlockSpec((1, D, tn), lambda l, j: (l, 0, j)),
                  pl.BlockSpec((1, 1, tn), lambda l, j: (l, 0, j))],
        out_specs=pl.BlockSpec((1, 8, tn), lambda l, j: (l, 0, j)),
        out_shape=jax.ShapeDtypeStruct((L, 8, W), F32),
        compiler_params=_cparams(("parallel", "parallel"), 2 * D * tn * 4 + D * tn * 2),
        name="modulation",
    )(cs, mod_w, mod_b.reshape(L, 1, W))


def _nmm_kernel(x_ref, g_ref, mod_ref, w_ref, *rest, shift, scale, emit_h):
    if emit_h:
        o_ref, h_ref, h_scr = rest
    else:
        o_ref, h_scr = rest

    @pl.when(pl.program_id(1) == 0)
    def _():
        x = x_ref[...]
        ms = jnp.mean(x * x, axis=-1, keepdims=True)
        y = x * lax.rsqrt(ms + EPS) * g_ref[...]
        h = y * (1.0 + mod_ref[0, scale:scale + 1, :]) + mod_ref[0, shift:shift + 1, :]
        h_scr[...] = h.astype(BF16)
        if emit_h:
            h_ref[...] = h_scr[...]

    o_ref[...] = jnp.dot(h_scr[...], w_ref[...], preferred_element_type=F32).astype(o_ref.dtype)


def _norm_mod_matmul(x, g, mod_l, w, *, shift, scale, n_rows, n_lat_rows, seq, emit_h, name):
    D = x.shape[1]
    cols = w.shape[1]
    nb = mod_l.shape[0] - 1
    tm, tn = ROW_TILE, (1024 if cols % 1024 == 0 else 512)
    grp = functools.partial(_group_of, tile_rows=tm, n_lat_rows=n_lat_rows, seq=seq, n_batch=nb)
    out_shape = [jax.ShapeDtypeStruct((n_rows, cols), BF16)]
    out_specs = [pl.BlockSpec((tm, tn), lambda i, j: (i, j))]
    if emit_h:
        out_shape.append(jax.ShapeDtypeStruct((n_rows, D), BF16))
        out_specs.append(pl.BlockSpec((tm, D), lambda i, j: (i, 0)))
    vmem = 2 * tm * D * 4 + 2 * D * tn * 2 + 2 * tm * tn * 2 + 3 * tm * D * 2 + 2 * tm * D * 4
    res = pl.pallas_call(
        functools.partial(_nmm_kernel, shift=shift, scale=scale, emit_h=emit_h),
        grid=(n_rows // tm, cols // tn),
        in_specs=[pl.BlockSpec((tm, D), lambda i, j: (i, 0)),
                  pl.BlockSpec((1, D), lambda i, j: (0, 0)),
                  pl.BlockSpec((1, N_MOD, D), lambda i, j: (grp(i), 0, 0)),
                  pl.BlockSpec((D, tn), lambda i, j: (0, j))],
        out_specs=out_specs,
        out_shape=out_shape,
        scratch_shapes=[pltpu.VMEM((tm, D), BF16)],
        compiler_params=_cparams(("parallel", "arbitrary"), vmem),
        name=name,
    )(x, g.reshape(1, D), mod_l, w)
    return res if emit_h else res[0]


def _da_lane_perm():
    j = np.arange(LANES)
    comp, parity, pair = (j // 32) % 2, j // 64, j % 32
    return comp * 64 + 2 * pair + parity


def _gq_lane_perm():
    j = np.arange(LANES)
    return 2 * (j % 64) + j // 64


def _qk_prep_kernel(p_ref, tab_ref, gain_ref, o_ref):
    lane = lax.broadcasted_iota(jnp.int32, (1, LANES), 1)
    comp0 = (lane % 64) < 32
    n_da = 2 * DA_HEADS
    for hd in range(QK_COLS // LANES):
        sl = slice(hd * LANES, (hd + 1) * LANES)
        x = p_ref[:, sl].astype(F32)
        xx = x * x
        tot = jnp.sum(xx, axis=-1, keepdims=True)
        if hd < n_da:
            s0 = jnp.sum(jnp.where(comp0, xx, 0.0), axis=-1, keepdims=True)
            ms = jnp.where(comp0, s0, tot - s0) * (1.0 / 64.0)
            cos, sin = tab_ref[0], tab_ref[1]
        else:
            ms = tot * (1.0 / 128.0)
            cos, sin = tab_ref[2], tab_ref[3]
        y = x * lax.rsqrt(ms + EPS) * gain_ref[hd:hd + 1, :]
        o_ref[:, sl] = (y * cos + pltpu.roll(y, 64, 1) * sin).astype(BF16)


def _qk_prep(p, rope_tab, gains, *, n_lat_rows, seq):
    n = p.shape[0]
    tm = ROW_TILE
    n_lat_tiles = n_lat_rows // tm
    rope_blk = lambda i: jnp.where(i >= n_lat_tiles, seq // tm, i % (seq // tm))
    return pl.pallas_call(
        _qk_prep_kernel,
        grid=(n // tm,),
        in_specs=[pl.BlockSpec((tm, QK_COLS), lambda i: (i, 0)),
                  pl.BlockSpec((4, tm, LANES), lambda i: (0, rope_blk(i), 0)),
                  pl.BlockSpec((QK_COLS // LANES, LANES), lambda i: (0, 0))],
        out_specs=pl.BlockSpec((tm, QK_COLS), lambda i: (i, 0)),
        out_shape=jax.ShapeDtypeStruct((n, QK_COLS), BF16),
        compiler_params=_cparams(("parallel",), 4 * tm * QK_COLS * 2 + 8 * tm * LANES * 4 + 6 * tm * LANES * 4),
        name="qk_prep",
    )(p, rope_tab, gains)


def _flash_kernel(q_ref, kl_ref, vl_ref, kc_ref, vc_ref, aux_ref, o_ref, *, diff, n_lat_chunks, nq_lat, tk,
                  ctx_queries):
    tq = q_ref.shape[0]
    lane = lax.broadcasted_iota(jnp.int32, (1, LANES), 1)
    if diff:
        q = q_ref[...]
        comp0 = (lane % 64) < 32
        zero = jnp.zeros_like(q)
        q2 = jnp.concatenate([jnp.where(comp0, q, zero), jnp.where(comp0, zero, q)], axis=0)
    else:
        q2 = jnp.concatenate([q_ref[:, :LANES], q_ref[:, LANES:]], axis=0)

    def scores(k):
        return lax.dot_general(q2, k, (((1,), (1,)), ((), ())), preferred_element_type=F32)

    def weighted(p, v):
        ones_col = jnp.broadcast_to(jnp.where(lane == 0, 1.0, 0.0).astype(BF16), v.shape)
        return jnp.dot(p.astype(BF16), jnp.concatenate([v, ones_col], axis=1), preferred_element_type=F32)

    def fold(carry, parts):
        m_new = functools.reduce(jnp.maximum, [jnp.max(s, axis=-1, keepdims=True) for s, _ in parts])
        if carry is not None:
            m_new = jnp.maximum(carry[0], m_new)
        acc = None
        for s, v in parts:
            pv = weighted(jnp.exp2(s - m_new), v)
            acc = pv if acc is None else acc + pv
        if carry is not None:
            acc = acc + jnp.exp2(carry[0] - m_new) * carry[1]
        return m_new, acc

    def finish(carry):
        acc = carry[1]
        o = acc[:, :LANES] / acc[:, LANES:LANES + 1]
        if diff:
            d = o[:tq] - aux_ref[0:1, :] * o[tq:]
            ms = jnp.mean(d * d, axis=-1, keepdims=True)
            o_ref[...] = (d * lax.rsqrt(ms + EPS) * aux_ref[1:2, :]).astype(o_ref.dtype)
        else:
            o_ref[:, :LANES] = o[:tq].astype(o_ref.dtype)
            o_ref[:, LANES:] = o[tq:].astype(o_ref.dtype)

    def latent_queries():
        carry = None
        for c in range(n_lat_chunks):
            parts = [(scores(kl_ref[c * tk:(c + 1) * tk, :]), vl_ref[c * tk:(c + 1) * tk, :])]
            if c == n_lat_chunks - 1:
                parts.append((scores(kc_ref[...]), vc_ref[...]))
            carry = fold(carry, parts)
        finish(carry)

    if ctx_queries:
        is_lat = pl.program_id(2) < nq_lat
        pl.when(is_lat)(latent_queries)

        @pl.when(jnp.logical_not(is_lat))
        def _():
            finish(fold(None, [(scores(kc_ref[...]), vc_ref[...])]))
    else:
        latent_queries()


def _attention(qk, p, aux, *, diff, n_batch, seq, ctx_len, with_ctx_queries):
    tq, tk = ATT_TQ, min(ATT_TK, seq)
    nq_lat, nq_ctx = seq // tq, ctx_len // tq
    n_lat_rows = n_batch * seq
    if diff:
        n_heads, qw = DA_HEADS, LANES
        q_blk = lambda h: h
        k_blk = lambda h: DA_HEADS + h
        v_blk = lambda h: V_DA_BLK + h
    else:
        n_heads, qw = GQ_KV, 2 * LANES
        q_blk = lambda h: (2 * DA_HEADS * LANES) // qw + h
        k_blk = lambda h: 2 * DA_HEADS + GQ_HEADS + h
        v_blk = lambda h: V_GQ_BLK + h
    nq = nq_lat + (nq_ctx if with_ctx_queries else 0)
    q_row = lambda b, qi: jnp.where(qi >= nq_lat, n_batch * nq_lat + b * nq_ctx + qi - nq_lat, b * nq_lat + qi)
    ctx_row = lambda b: n_lat_rows // ctx_len + b
    kernel = functools.partial(_flash_kernel, diff=diff, n_lat_chunks=seq // tk, nq_lat=nq_lat, tk=tk,
                               ctx_queries=with_ctx_queries)
    n_out = n_lat_rows + (n_batch * ctx_len if with_ctx_queries else 0)
    vmem = 4 * seq * LANES * 2 + 2 * tq * (tk + ctx_len) * 4 * 4 + 8 * tq * LANES * 4 * 3
    return pl.pallas_call(
        kernel,
        grid=(n_batch, n_heads, nq),
        in_specs=[pl.BlockSpec((tq, qw), lambda b, h, qi: (q_row(b, qi), q_blk(h))),
                  pl.BlockSpec((seq, LANES), lambda b, h, qi: (b, k_blk(h))),
                  pl.BlockSpec((seq, LANES), lambda b, h, qi: (b, v_blk(h))),
                  pl.BlockSpec((ctx_len, LANES), lambda b, h, qi: (ctx_row(b), k_blk(h))),
                  pl.BlockSpec((ctx_len, LANES), lambda b, h, qi: (ctx_row(b), v_blk(h))),
                  pl.BlockSpec((8, LANES), lambda b, h, qi: (0, 0))],
        out_specs=pl.BlockSpec((tq, qw), lambda b, h, qi: (q_row(b, qi), h)),
        out_shape=jax.ShapeDtypeStruct((n_out, n_heads * qw), BF16),
        compiler_params=_cparams(("parallel", "parallel", "parallel"), vmem),
        name="diff_attention" if diff else "gqa_attention",
    )(qk, qk, p, qk, p, aux)


def _conv_kernel(cur_ref, prev_ref, next_ref, w_ref, o_ref, z_scr, u_scr, *, tiles_lat, tiles_ctx, n_lat_tiles):
    t = pl.program_id(0)
    in_lat = t < n_lat_tiles
    pos = jnp.where(in_lat, t % tiles_lat, (t - n_lat_tiles) % tiles_ctx)
    last = jnp.where(in_lat, tiles_lat, tiles_ctx) - 1
    has_prev = (pos > 0).astype(F32)
    has_next = (pos < last).astype(F32)
    T, H = cur_ref.shape[0], CONV_HALO
    W = 512

    def cols(ref, k):
        return ref[:, k * W:(k + 1) * W].astype(F32)

    def glu(ref):
        return cols(ref, 3) * (1.0 / (1.0 + jnp.exp(-cols(ref, 4))))

    z_scr[0:H, :] = cols(prev_ref, 1) * cols(prev_ref, 2) * has_prev
    z_scr[H:H + T, :] = cols(cur_ref, 1) * cols(cur_ref, 2)
    z_scr[H + T:, :] = cols(next_ref, 1) * cols(next_ref, 2) * has_next
    acc = jnp.zeros((T, W), F32)
    for k in range(SC_K):
        off = H + k - SC_K // 2
        acc = acc + w_ref[k:k + 1, :] * z_scr[off:off + T, :]
    o_ref[:, :W] = (cols(cur_ref, 0) * acc).astype(o_ref.dtype)

    u_scr[0:H, :] = glu(prev_ref) * has_prev
    u_scr[H:H + T, :] = glu(cur_ref)
    u_scr[H + T:, :] = glu(next_ref) * has_next
    acc = jnp.zeros((T, W), F32)
    for k in range(CF_K):
        off = H + k - CF_K // 2
        acc = acc + w_ref[SC_K + k:SC_K + k + 1, :] * u_scr[off:off + T, :]
    r = SC_K + CF_K
    u = acc + w_ref[r:r + 1, :]
    mu = jnp.mean(u, axis=-1, keepdims=True)
    var = jnp.mean(jnp.square(u - mu), axis=-1, keepdims=True)
    y = (u - mu) * lax.rsqrt(var + EPS) * w_ref[r + 1:r + 2, :] + w_ref[r + 2:r + 3, :]
    o_ref[:, W:] = (y * (1.0 / (1.0 + jnp.exp(-y)))).astype(o_ref.dtype)


def _conv_mixers(p, conv_w, *, n_rows, n_lat_rows, seq, ctx_len):
    T, H = CONV_TILE, CONV_HALO
    r = T // H
    n_halo_blocks = p.shape[0] // H
    cblk = CONV_COL0 // CONV_COLS
    kernel = functools.partial(_conv_kernel, tiles_lat=seq // T, tiles_ctx=ctx_len // T, n_lat_tiles=n_lat_rows // T)
    return pl.pallas_call(
        kernel,
        grid=(n_rows // T,),
        in_specs=[pl.BlockSpec((T, CONV_COLS), lambda t: (t, cblk)),
                  pl.BlockSpec((H, CONV_COLS), lambda t: (jnp.maximum(t * r - 1, 0), cblk)),
                  pl.BlockSpec((H, CONV_COLS), lambda t: (jnp.minimum((t + 1) * r, n_halo_blocks - 1), cblk)),
                  pl.BlockSpec(conv_w.shape, lambda t: (0, 0))],
        out_specs=pl.BlockSpec((T, 1024), lambda t: (t, 0)),
        out_shape=jax.ShapeDtypeStruct((n_rows, 1024), BF16),
        scratch_shapes=[pltpu.VMEM((T + 2 * H, 512), F32), pltpu.VMEM((T + 2 * H, 512), F32)],
        compiler_params=_cparams(("parallel",), 4 * T * CONV_COLS * 2 + 16 * T * 512 * 4),
        name="conv_mixers",
    )(p, p, p, conv_w)


def _merge_kernel(ya_ref, yb_ref, yc_ref, g_ref, x_ref, mod_ref, wbr_ref, wo_ref, o_ref):
    D = x_ref.shape[1]
    W = 512
    ys = (ya_ref[...], yb_ref[...], yc_ref[:, :W], yc_ref[:, W:])
    merged = jnp.zeros(x_ref.shape, F32)
    for i in range(N_BRANCH):
        z = jnp.dot(ys[i], wbr_ref[i], preferred_element_type=F32)
        gate = g_ref[:, i * D:(i + 1) * D].astype(F32)
        merged = merged + z * (1.0 / (1.0 + jnp.exp(-gate)))
    out = jnp.dot(merged.astype(BF16), wo_ref[...], preferred_element_type=F32)
    o_ref[...] = x_ref[...] + mod_ref[0, 2:3, :] * out


def _merge(ya, yb, yc, gates, x, mod_l, wbr, wo, *, n_rows, n_lat_rows, seq):
    D = x.shape[1]
    nb = mod_l.shape[0] - 1
    tm = CONV_TILE
    grp = functools.partial(_group_of, tile_rows=tm, n_lat_rows=n_lat_rows, seq=seq, n_batch=nb)
    vmem = (2 * tm * N_BRANCH * D * 2 + 4 * tm * D * 4 + N_BRANCH * 512 * D * 2 + D * D * 2
            + 4 * tm * D * 4 + 2 * tm * 2048 * 2)
    return pl.pallas_call(
        _merge_kernel,
        grid=(n_rows // tm,),
        in_specs=[pl.BlockSpec((tm, 512), lambda i: (i, 0)),
                  pl.BlockSpec((tm, 512), lambda i: (i, 0)),
                  pl.BlockSpec((tm, 1024), lambda i: (i, 0)),
                  pl.BlockSpec((tm, N_BRANCH * D), lambda i: (i, 0)),
                  pl.BlockSpec((tm, D), lambda i: (i, 0)),
                  pl.BlockSpec((1, N_MOD, D), lambda i: (grp(i), 0, 0)),
                  pl.BlockSpec(wbr.shape, lambda i: (0, 0, 0), pipeline_mode=pl.Buffered(1)),
                  pl.BlockSpec(wo.shape, lambda i: (0, 0), pipeline_mode=pl.Buffered(1))],
        out_specs=pl.BlockSpec((tm, D), lambda i: (i, 0)),
        out_shape=jax.ShapeDtypeStruct((n_rows, D), F32),
        compiler_params=_cparams(("parallel",), vmem),
        name="merge_branches",
    )(ya, yb, yc, gates, x, mod_l, wbr, wo)


def _top_values(work, n):
    R = work.shape[0]
    idx = lax.broadcasted_iota(jnp.int32, work.shape, 0).astype(F32)
    tops = []
    for _ in range(n):
        m = jnp.max(work, axis=0, keepdims=True)
        tops.append(m)
        first = jnp.min(jnp.where(work == m, idx, float(R)), axis=0, keepdims=True)
        work = jnp.where(idx == first, -jnp.inf, work)
    return tops


def _stack_rows(rows):
    n, T = len(rows), rows[0].shape[1]
    idx = lax.broadcasted_iota(jnp.int32, (n, T), 0)
    out = jnp.zeros((n, T), F32)
    for r, row in enumerate(rows):
        out = jnp.where(idx == r, row, out)
    return out


def _peer_select_kernel(q_ref, k1_ref, k2_ref, s1_ref, s2_ref, b_ref, c_ref, tau_ref):
    K = PK_TOPK
    nt = (((1,), (1,)), ((), ()))
    s1 = lax.dot_general(k1_ref[...], q_ref[:, :N_KEYS], nt, preferred_element_type=F32)
    s2 = lax.dot_general(k2_ref[...], q_ref[:, N_KEYS:], nt, preferred_element_type=F32)
    t1 = _top_values(s1, K)
    t2 = _top_values(s2, K)
    T1, T2 = _stack_rows(t1), _stack_rows(t2)
    ninf = jnp.full((8, s1.shape[1]), -jnp.inf, F32)
    r8 = lax.broadcasted_iota(jnp.int32, (8, s1.shape[1]), 0)
    slabs = [t1[0] + T2[0:8], t1[0] + T2[8:16], t1[1] + T2[0:8]]
    slabs.append(jnp.where(r8 < K // 3, t1[2] + T2[0:8], ninf))
    slabs.append(jnp.where(r8 < K // 4, t1[3] + T2[0:8], ninf))
    slabs.append(T1[8:16] + t2[0])
    slabs.append(jnp.where(r8 >= 4, T1[0:8] + t2[0], ninf))
    slabs.append(jnp.where(r8 >= 4, T1[0:8] + t2[1], ninf))
    slabs.append(jnp.where(r8 == 4, T1[0:8] + t2[2], ninf))
    cand = jnp.concatenate(slabs, axis=0)
    best = _top_values(cand, K)
    top, tau = best[0], best[K - 1]
    z = jnp.sum(jnp.where(cand >= tau, jnp.exp(cand - top), 0.0), axis=0, keepdims=True)
    s1_ref[0] = s1
    s2_ref[0] = s2
    b_ref[0] = jnp.exp(s1 - t1[0]) / z
    c_ref[0] = jnp.exp(s2 - t2[0])
    tau_ref[0] = tau


def _peer_select(q, k1, k2):
    n = q.shape[0]
    tm = ROW_TILE
    big = jax.ShapeDtypeStruct((PK_HEADS, N_KEYS, n), F32)
    big_spec = pl.BlockSpec((1, N_KEYS, tm), lambda i, h: (h, 0, i))
    return pl.pallas_call(
        _peer_select_kernel,
        grid=(n // tm, PK_HEADS),
        in_specs=[pl.BlockSpec((tm, 2 * N_KEYS), lambda i, h: (i, h)),
                  pl.BlockSpec(k1.shape, lambda i, h: (0, 0)),
                  pl.BlockSpec(k2.shape, lambda i, h: (0, 0))],
        out_specs=[big_spec, big_spec, big_spec, big_spec, pl.BlockSpec((1, 1, tm), lambda i, h: (h, 0, i))],
        out_shape=[big, big, big, big, jax.ShapeDtypeStruct((PK_HEADS, 1, n), F32)],
        compiler_params=_cparams(("parallel", "parallel"), 8 * N_KEYS * tm * 4 + 32 * N_KEYS * tm * 4),
        name="peer_select",
    )(q, k1, k2)


def _gelu_tanh(x):
    return 0.5 * x * (1.0 + jnp.tanh(math.sqrt(2.0 / math.pi) * (x + 0.044715 * (x * x * x))))


def _peer_dense_kernel(h_ref, u_ref, vt_ref, s1r_ref, br_ref, s2_ref, c_ref, tau_ref, x_ref, mod_ref, o_ref,
                       acc_scr, s0_scr, s1_scr, w0_scr, w1_scr, *, nj):
    t = pl.program_id(0)
    tm = h_ref.shape[0]
    jc = jnp.maximum(t - 2, 0) % nj

    @pl.when(t == 0)
    def _():
        for ref in (s0_scr, s1_scr, w0_scr, w1_scr):
            ref[...] = jnp.zeros(ref.shape, ref.dtype)

    @pl.when(jc == 0)
    def _():
        acc_scr[...] = jnp.zeros(acc_scr.shape, F32)

    def tick(s_write, s_read, w_write, w_read):
        n_slabs = u_ref.shape[0] // N_KEYS
        kq = N_KEYS // 4

        def stage_b(quarter, lc):
            keys = slice(quarter * kq, (quarter + 1) * kq)
            cols = slice(lc * LANES, (lc + 1) * LANES)
            gates = [None] * n_slabs
            for hd in range(PK_HEADS):
                s2, cf, tau = s2_ref[hd, keys, cols], c_ref[hd, keys, cols], tau_ref[hd, :, cols]
                for b in range(n_slabs):
                    total = s2 + s1r_ref[b, hd:hd + 1, cols]
                    term = jnp.where(total >= tau, cf * br_ref[b, hd:hd + 1, cols], 0.0)
                    gates[b] = term if gates[b] is None else gates[b] + term
            for b in range(n_slabs):
                rows = slice(b * N_KEYS + keys.start, b * N_KEYS + keys.stop)
                w_write[rows, cols] = (gates[b] * _gelu_tanh(s_read[rows, cols])).astype(BF16)

        def stage_a(half):
            cols = slice(half * (tm // 2), (half + 1) * (tm // 2))
            s_write[:, cols] = lax.dot_general(u_ref[...], h_ref[cols, :], (((1,), (1,)), ((), ())),
                                               preferred_element_type=F32)

        def stage_c(quarter):
            rows = slice(quarter * (acc_scr.shape[0] // 4), (quarter + 1) * (acc_scr.shape[0] // 4))
            acc_scr[rows, :] += jnp.dot(vt_ref[rows, :], w_read[...], preferred_element_type=F32)

        units = [(quarter, lc) for quarter in range(4) for lc in range(tm // LANES)]
        plan = [(stage_a, 0, 4), (stage_c, 0, 2), (stage_c, 1, 2), (stage_a, 1, 4), (stage_c, 2, 2), (stage_c, 3, 2)]
        assert sum(n for _, _, n in plan) == len(units)
        for mxu_stage, arg, n_units in plan:
            mxu_stage(arg)
            for _ in range(n_units):
                stage_b(*units.pop(0))

    parity = t % 2
    pl.when(parity == 0)(functools.partial(tick, s0_scr, s1_scr, w1_scr, w0_scr))
    pl.when(parity == 1)(functools.partial(tick, s1_scr, s0_scr, w0_scr, w1_scr))

    @pl.when(jnp.logical_and(jc == nj - 1, t >= 2))
    def _():
        o_ref[...] = x_ref[...] + mod_ref[0, 5:6, :] * acc_scr[...].T


def _peer_dense(h, u, vt, s1r, br, s2, c, tau, x, mod_l, *, n_rows, n_lat_rows, seq):
    D = x.shape[1]
    n_exp = u.shape[0]
    nb = mod_l.shape[0] - 1
    tm, te = ROW_TILE, PEER_TE
    nk = te // N_KEYS
    nj = n_exp // te
    n_items = (n_rows // tm) * nj
    grp = functools.partial(_group_of, tile_rows=tm, n_lat_rows=n_lat_rows, seq=seq, n_batch=nb)

    def item(t, lag):
        k = jnp.clip(t - lag, 0, n_items - 1)
        return k // nj, k % nj

    sel_spec = pl.BlockSpec((PK_HEADS, N_KEYS, tm), lambda t: (0, 0, item(t, 1)[0]))
    row_spec = pl.BlockSpec((nk, PK_HEADS, tm), lambda t: (item(t, 1)[1], 0, item(t, 1)[0]))
    vmem = (2 * tm * D * 2 + 4 * te * D * 2 + 4 * PK_HEADS * N_KEYS * tm * 4 + 4 * tm * D * 4
            + D * tm * 4 + 2 * te * tm * 6 + 2 * D * tm * 4)
    return pl.pallas_call(
        functools.partial(_peer_dense_kernel, nj=nj),
        grid=(n_items + 2,),
        in_specs=[pl.BlockSpec((tm, D), lambda t: (item(t, 0)[0], 0)),
                  pl.BlockSpec((te, D), lambda t: (item(t, 0)[1], 0)),
                  pl.BlockSpec((D, te), lambda t: (0, item(t, 2)[1])),
                  row_spec, row_spec, sel_spec, sel_spec,
                  pl.BlockSpec((PK_HEADS, 1, tm), lambda t: (0, 0, item(t, 1)[0])),
                  pl.BlockSpec((tm, D), lambda t: (item(t, 2)[0], 0)),
                  pl.BlockSpec((1, N_MOD, D), lambda t: (grp(item(t, 2)[0]), 0, 0))],
        out_specs=pl.BlockSpec((tm, D), lambda t: (item(t, 2)[0], 0)),
        out_shape=jax.ShapeDtypeStruct((n_rows, D), F32),
        scratch_shapes=[pltpu.VMEM((D, tm), F32), pltpu.VMEM((te, tm), F32), pltpu.VMEM((te, tm), F32),
                        pltpu.VMEM((te, tm), BF16), pltpu.VMEM((te, tm), BF16)],
        compiler_params=_cparams(("arbitrary",), vmem),
        name="peer_dense",
    )(h, u, vt, s1r, br, s2, c, tau, x, mod_l)


def _rope_tables(seq, pad_rows):
    t = jnp.arange(seq, dtype=jnp.int32)
    row = (t // GRID_W).astype(F32)
    col = (t % GRID_W).astype(F32)

    def tables(dim, lanes_per_pair):
        n_pair = dim // 4
        inv = ROPE_THETA ** (-jnp.arange(n_pair, dtype=F32) / n_pair)
        ang = jnp.concatenate([row[:, None] * inv, col[:, None] * inv], axis=-1)
        lane_pair = np.arange(LANES) % lanes_per_pair
        sign = np.where(np.arange(LANES) < 64, -1.0, 1.0).astype(np.float32)
        return jnp.cos(ang)[:, lane_pair], jnp.sin(ang)[:, lane_pair] * sign

    ca, sa = tables(64, 32)
    cb, sb = tables(128, 64)
    tab = jnp.stack([ca, sa, cb, sb])
    ident = jnp.stack([jnp.ones((pad_rows, LANES), F32), jnp.zeros((pad_rows, LANES), F32)] * 2)
    return jnp.concatenate([tab, ident], axis=1)


def _permuted_proj_weight(w):
    D = w.shape[0]

    def da(blk):
        return blk.reshape(D, DA_HEADS, 2, 32, 2).transpose(0, 1, 4, 2, 3).reshape(D, DA_HEADS * LANES)

    def gq(blk):
        return blk.reshape(D, -1, 64, 2).transpose(0, 1, 3, 2).reshape(D, blk.shape[1])

    return jnp.concatenate([da(w[:, 0:512]), da(w[:, 512:1024]), gq(w[:, 1536:2048]), gq(w[:, 2048:2304]),
                            w[:, 1024:1536], w[:, 2304:5120]], axis=1)


def _qk_gains(da_qn_g, da_kn_g, gq_qn_g, gq_kn_g):
    da, gq = _da_lane_perm() % 64, _gq_lane_perm()
    log2e = math.log2(math.e)
    rows = ([da_qn_g[da] * (64 ** -0.5 * log2e)] * DA_HEADS + [da_kn_g[da]] * DA_HEADS
            + [gq_qn_g[gq] * (128 ** -0.5 * log2e)] * GQ_HEADS + [gq_kn_g[gq]] * GQ_KV)
    return jnp.stack(rows).astype(F32)


def kernel(x, c, ctx, c_ctx, mod_w, mod_b, norm1_g, norm2_g, w_in, da_qn_g, da_kn_g, da_lam, da_head_g,
           gq_qn_g, gq_kn_g, sc_w, cf_w, cf_b, cf_ln_g, cf_ln_b, w_branch, w_out,
           pk_wq, pk_k1, pk_k2, pk_u, pk_v):
    B, S, D = x.shape
    C = ctx.shape[1]
    L = mod_w.shape[0]
    n_lat, n_all = B * S, B * S + B * C
    assert S % ROW_TILE == 0 and (B * C) % ROW_TILE == 0 and C % CONV_TILE == 0 and S % min(ATT_TK, S) == 0
    assert B + 1 <= 8 and D == N_BRANCH * 512 and w_in.shape[2] == 5120 + N_BRANCH * D

    xs = jnp.concatenate([x.reshape(n_lat, D), ctx.reshape(B * C, D)], axis=0)
    cs = jnp.zeros((8, D), F32).at[:B].set(c).at[B].set(c_ctx)
    mod = _modulation(cs, mod_w, mod_b).reshape(L, 8, N_MOD, D)[:, :B + 1]
    rope_tab = _rope_tables(S, ROW_TILE)
    geom = dict(n_lat_rows=n_lat, seq=S)

    for l in range(L):
        with_ctx = l < L - 1
        n_rows = n_all if with_ctx else n_lat
        lam_init = 0.8 - 0.6 * math.exp(-0.3 * l)
        lp = da_lam[l].astype(F32)
        lam = jnp.exp(jnp.sum(lp[0] * lp[1])) - jnp.exp(jnp.sum(lp[2] * lp[3])) + lam_init
        aux = jnp.zeros((8, LANES), F32).at[0].set(lam).at[1].set(da_head_g[l] * (1.0 - lam_init))
        w_p = _permuted_proj_weight(w_in[l][:, :5120]).astype(BF16)
        w_g = w_in[l][:, 5120:].astype(BF16)
        conv_w = jnp.concatenate([sc_w[l], cf_w[l], cf_b[l][None], cf_ln_g[l][None], cf_ln_b[l][None],
                                  jnp.zeros((3, 512), F32)], axis=0)
        mod_l = mod[l]

        p = _norm_mod_matmul(xs, norm1_g[l], mod_l, w_p, shift=0, scale=1, n_rows=n_all, emit_h=False,
                             name="in_proj", **geom)
        gates = _norm_mod_matmul(xs, norm1_g[l], mod_l, w_g, shift=0, scale=1, n_rows=n_rows, emit_h=False,
                                 name="gate_proj", **geom)
        qk = _qk_prep(p, rope_tab, _qk_gains(da_qn_g[l], da_kn_g[l], gq_qn_g[l], gq_kn_g[l]), **geom)
        att = dict(n_batch=B, seq=S, ctx_len=C, with_ctx_queries=with_ctx)
        ya = _attention(qk, p, aux, diff=True, **att)
        yb = _attention(qk, p, aux, diff=False, **att)
        yc = _conv_mixers(p, conv_w, n_rows=n_rows, ctx_len=C, **geom)
        xs = _merge(ya, yb, yc, gates, xs, mod_l, w_branch[l].astype(BF16), w_out[l].astype(BF16),
                    n_rows=n_rows, **geom)

        q, h2 = _norm_mod_matmul(xs, norm2_g[l], mod_l, pk_wq[l].astype(BF16), shift=3, scale=4, n_rows=n_rows,
                                 emit_h=True, name="peer_query", **geom)
        s1, s2, bf, cf, tau = _peer_select(q, pk_k1[l].astype(BF16), pk_k2[l].astype(BF16))
        s1r, br = jnp.transpose(s1, (1, 0, 2)), jnp.transpose(bf, (1, 0, 2))
        xs = _peer_dense(h2, pk_u[l].astype(BF16), pk_v[l].T.astype(BF16), s1r, br, s2, cf, tau, xs, mod_l,
                         n_rows=n_rows, **geom)
    return xs[:n_lat].reshape(B, S, D)
```

```python
import functools
import math

import numpy as np
import jax
import jax.numpy as jnp
from jax import lax
from jax.experimental import pallas as pl
from jax.experimental.pallas import tpu as pltpu

F32 = jnp.float32
BF16 = jnp.bfloat16

EPS = 1e-6
N_MOD = 6
GRID_W = 64
ROPE_THETA = 10000.0
N_BRANCH = 4
DA_HEADS = 4
GQ_HEADS = 4
GQ_KV = 2
SC_K = 3
CF_K = 31
PK_HEADS = 8
N_KEYS = 128
PK_TOPK = 16

LANES = 128
V7X_VMEM_BYTES = 64 * 1024 * 1024
VMEM_CAP = V7X_VMEM_BYTES - 8 * 1024 * 1024

ROW_TILE = 512
CONV_TILE = 256
CONV_HALO = 16
ATT_TQ = 256
ATT_TK = 2048
PEER_TE = 512

QK_COLS = (2 * DA_HEADS + GQ_HEADS + GQ_KV) * LANES
V_DA_BLK = QK_COLS // LANES
V_GQ_BLK = V_DA_BLK + DA_HEADS
CONV_COL0 = (V_GQ_BLK + GQ_KV) * LANES
CONV_COLS = 5 * 512
P_COLS = CONV_COL0 + CONV_COLS


def _cparams(sem, vmem_bytes):
    limit = int(min(max(vmem_bytes * 5 // 4 + (4 << 20), 16 << 20), VMEM_CAP))
    return pltpu.CompilerParams(dimension_semantics=sem, vmem_limit_bytes=limit)


def _group_of(tile, tile_rows, n_lat_rows, seq, n_batch):
    n_lat_tiles = n_lat_rows // tile_rows
    return jnp.where(tile >= n_lat_tiles, n_batch, tile // (seq // tile_rows))


def _mod_kernel(c_ref, w_ref, b_ref, o_ref):
    c = c_ref[...]
    sc = c * (1.0 / (1.0 + jnp.exp(-c)))
    o_ref[0] = jnp.dot(sc.astype(BF16), w_ref[0].astype(BF16), preferred_element_type=F32) + b_ref[0]


def _modulation(cs, mod_w, mod_b):
    L, D, W = mod_w.shape
    tn = 1024
    return pl.pallas_call(
        _mod_kernel,
        grid=(L, W // tn),
        in_specs=[pl.BlockSpec((8, D), lambda l, j: (0, 0)),
                  pl.BlockSpec((1, D, tn), lambda l, j: (l, 0, j)),
                  pl.BlockSpec((1, 1, tn), lambda l, j: (l, 0, j))],
        out_specs=pl.BlockSpec((1, 8, tn), lambda l, j: (l, 0, j)),
        out_shape=jax.ShapeDtypeStruct((L, 8, W), F32),
        compiler_params=_cparams(("parallel", "parallel"), 2 * D * tn * 4 + D * tn * 2),
        name="modulation",
    )(cs, mod_w, mod_b.reshape(L, 1, W))


def _nmm_kernel(x_ref, g_ref, mod_ref, w_ref, *rest, shift, scale, emit_h):
    if emit_h:
        o_ref, h_ref, h_scr = rest
    else:
        o_ref, h_scr = rest

    @pl.when(pl.program_id(1) == 0)
    def _():
        x = x_ref[...]
        ms = jnp.mean(x * x, axis=-1, keepdims=True)
        y = x * lax.rsqrt(ms + EPS) * g_ref[...]
        h = y * (1.0 + mod_ref[0, scale:scale + 1, :]) + mod_ref[0, shift:shift + 1, :]
        h_scr[...] = h.astype(BF16)
        if emit_h:
            h_ref[...] = h_scr[...]

    o_ref[...] = jnp.dot(h_scr[...], w_ref[...], preferred_element_type=F32).astype(o_ref.dtype)


def _norm_mod_matmul(x, g, mod_l, w, *, shift, scale, n_rows, n_lat_rows, seq, emit_h, name):
    D = x.shape[1]
    cols = w.shape[1]
    nb = mod_l.shape[0] - 1
    tm, tn = ROW_TILE, (1024 if cols % 1024 == 0 else 512)
    grp = functools.partial(_group_of, tile_rows=tm, n_lat_rows=n_lat_rows, seq=seq, n_batch=nb)
    out_shape = [jax.ShapeDtypeStruct((n_rows, cols), BF16)]
    out_specs = [pl.BlockSpec((tm, tn), lambda i, j: (i, j))]
    if emit_h:
        out_shape.append(jax.ShapeDtypeStruct((n_rows, D), BF16))
        out_specs.append(pl.BlockSpec((tm, D), lambda i, j: (i, 0)))
    vmem = 2 * tm * D * 4 + 2 * D * tn * 2 + 2 * tm * tn * 2 + 3 * tm * D * 2 + 2 * tm * D * 4
    res = pl.pallas_call(
        functools.partial(_nmm_kernel, shift=shift, scale=scale, emit_h=emit_h),
        grid=(n_rows // tm, cols // tn),
        in_specs=[pl.BlockSpec((tm, D), lambda i, j: (i, 0)),
                  pl.BlockSpec((1, D), lambda i, j: (0, 0)),
                  pl.BlockSpec((1, N_MOD, D), lambda i, j: (grp(i), 0, 0)),
                  pl.BlockSpec((D, tn), lambda i, j: (0, j))],
        out_specs=out_specs,
        out_shape=out_shape,
        scratch_shapes=[pltpu.VMEM((tm, D), BF16)],
        compiler_params=_cparams(("parallel", "arbitrary"), vmem),
        name=name,
    )(x, g.reshape(1, D), mod_l, w)
    return res if emit_h else res[0]


def _da_lane_perm():
    j = np.arange(LANES)
    comp, parity, pair = (j // 32) % 2, j // 64, j % 32
    return comp * 64 + 2 * pair + parity


def _gq_lane_perm():
    j = np.arange(LANES)
    return 2 * (j % 64) + j // 64


def _qk_prep_kernel(p_ref, tab_ref, gain_ref, o_ref):
    lane = lax.broadcasted_iota(jnp.int32, (1, LANES), 1)
    comp0 = (lane % 64) < 32
    n_da = 2 * DA_HEADS
    for hd in range(QK_COLS // LANES):
        sl = slice(hd * LANES, (hd + 1) * LANES)
        x = p_ref[:, sl].astype(F32)
        xx = x * x
        tot = jnp.sum(xx, axis=-1, keepdims=True)
        if hd < n_da:
            s0 = jnp.sum(jnp.where(comp0, xx, 0.0), axis=-1, keepdims=True)
            ms = jnp.where(comp0, s0, tot - s0) * (1.0 / 64.0)
            cos, sin = tab_ref[0], tab_ref[1]
        else:
            ms = tot * (1.0 / 128.0)
            cos, sin = tab_ref[2], tab_ref[3]
        y = x * lax.rsqrt(ms + EPS) * gain_ref[hd:hd + 1, :]
        o_ref[:, sl] = (y * cos + pltpu.roll(y, 64, 1) * sin).astype(BF16)


def _qk_prep(p, rope_tab, gains, *, n_lat_rows, seq):
    n = p.shape[0]
    tm = ROW_TILE
    n_lat_tiles = n_lat_rows // tm
    rope_blk = lambda i: jnp.where(i >= n_lat_tiles, seq // tm, i % (seq // tm))
    return pl.pallas_call(
        _qk_prep_kernel,
        grid=(n // tm,),
        in_specs=[pl.BlockSpec((tm, QK_COLS), lambda i: (i, 0)),
                  pl.BlockSpec((4, tm, LANES), lambda i: (0, rope_blk(i), 0)),
                  pl.BlockSpec((QK_COLS // LANES, LANES), lambda i: (0, 0))],
        out_specs=pl.BlockSpec((tm, QK_COLS), lambda i: (i, 0)),
        out_shape=jax.ShapeDtypeStruct((n, QK_COLS), BF16),
        compiler_params=_cparams(("parallel",), 4 * tm * QK_COLS * 2 + 8 * tm * LANES * 4 + 6 * tm * LANES * 4),
        name="qk_prep",
    )(p, rope_tab, gains)


def _flash_kernel(q_ref, kl_ref, vl_ref, kc_ref, vc_ref, aux_ref, o_ref, *, diff, n_lat_chunks, nq_lat, tk,
                  ctx_queries):
    tq = q_ref.shape[0]
    lane = lax.broadcasted_iota(jnp.int32, (1, LANES), 1)
    if diff:
        q = q_ref[...]
        comp0 = (lane % 64) < 32
        zero = jnp.zeros_like(q)
        q2 = jnp.concatenate([jnp.where(comp0, q, zero), jnp.where(comp0, zero, q)], axis=0)
    else:
        q2 = jnp.concatenate([q_ref[:, :LANES], q_ref[:, LANES:]], axis=0)

    def scores(k):
        return lax.dot_general(q2, k, (((1,), (1,)), ((), ())), preferred_element_type=F32)

    def weighted(p, v):
        ones_col = jnp.broadcast_to(jnp.where(lane == 0, 1.0, 0.0).astype(BF16), v.shape)
        return jnp.dot(p.astype(BF16), jnp.concatenate([v, ones_col], axis=1), preferred_element_type=F32)

    def fold(carry, parts):
        m_new = functools.reduce(jnp.maximum, [jnp.max(s, axis=-1, keepdims=True) for s, _ in parts])
        if carry is not None:
            m_new = jnp.maximum(carry[0], m_new)
        acc = None
        for s, v in parts:
            pv = weighted(jnp.exp2(s - m_new), v)
            acc = pv if acc is None else acc + pv
        if carry is not None:
            acc = acc + jnp.exp2(carry[0] - m_new) * carry[1]
        return m_new, acc

    def finish(carry):
        acc = carry[1]
        o = acc[:, :LANES] / acc[:, LANES:LANES + 1]
        if diff:
            d = o[:tq] - aux_ref[0:1, :] * o[tq:]
            ms = jnp.mean(d * d, axis=-1, keepdims=True)
            o_ref[...] = (d * lax.rsqrt(ms + EPS) * aux_ref[1:2, :]).astype(o_ref.dtype)
        else:
            o_ref[:, :LANES] = o[:tq].astype(o_ref.dtype)
            o_ref[:, LANES:] = o[tq:].astype(o_ref.dtype)

    def latent_queries():
        carry = None
        for c in range(n_lat_chunks):
            parts = [(scores(kl_ref[c * tk:(c + 1) * tk, :]), vl_ref[c * tk:(c + 1) * tk, :])]
            if c == n_lat_chunks - 1:
                parts.append((scores(kc_ref[...]), vc_ref[...]))
            carry = fold(carry, parts)
        finish(carry)

    if ctx_queries:
        is_lat = pl.program_id(2) < nq_lat
        pl.when(is_lat)(latent_queries)

        @pl.when(jnp.logical_not(is_lat))
        def _():
            finish(fold(None, [(scores(kc_ref[...]), vc_ref[...])]))
    else:
        latent_queries()


def _attention(qk, p, aux, *, diff, n_batch, seq, ctx_len, with_ctx_queries):
    tq, tk = ATT_TQ, min(ATT_TK, seq)
    nq_lat, nq_ctx = seq // tq, ctx_len // tq
    n_lat_rows = n_batch * seq
    if diff:
        n_heads, qw = DA_HEADS, LANES
        q_blk = lambda h: h
        k_blk = lambda h: DA_HEADS + h
        v_blk = lambda h: V_DA_BLK + h
    else:
        n_heads, qw = GQ_KV, 2 * LANES
        q_blk = lambda h: (2 * DA_HEADS * LANES) // qw + h
        k_blk = lambda h: 2 * DA_HEADS + GQ_HEADS + h
        v_blk = lambda h: V_GQ_BLK + h
    nq = nq_lat + (nq_ctx if with_ctx_queries else 0)
    q_row = lambda b, qi: jnp.where(qi >= nq_lat, n_batch * nq_lat + b * nq_ctx + qi - nq_lat, b * nq_lat + qi)
    ctx_row = lambda b: n_lat_rows // ctx_len + b
    kernel = functools.partial(_flash_kernel, diff=diff, n_lat_chunks=seq // tk, nq_lat=nq_lat, tk=tk,
                               ctx_queries=with_ctx_queries)
    n_out = n_lat_rows + (n_batch * ctx_len if with_ctx_queries else 0)
    vmem = 4 * seq * LANES * 2 + 2 * tq * (tk + ctx_len) * 4 * 4 + 8 * tq * LANES * 4 * 3
    return pl.pallas_call(
        kernel,
        grid=(n_batch, n_heads, nq),
        in_specs=[pl.BlockSpec((tq, qw), lambda b, h, qi: (q_row(b, qi), q_blk(h))),
                  pl.BlockSpec((seq, LANES), lambda b, h, qi: (b, k_blk(h))),
                  pl.BlockSpec((seq, LANES), lambda b, h, qi: (b, v_blk(h))),
                  pl.BlockSpec((ctx_len, LANES), lambda b, h, qi: (ctx_row(b), k_blk(h))),
                  pl.BlockSpec((ctx_len, LANES), lambda b, h, qi: (ctx_row(b), v_blk(h))),
                  pl.BlockSpec((8, LANES), lambda b, h, qi: (0, 0))],
        out_specs=pl.BlockSpec((tq, qw), lambda b, h, qi: (q_row(b, qi), h)),
        out_shape=jax.ShapeDtypeStruct((n_out, n_heads * qw), BF16),
        compiler_params=_cparams(("parallel", "parallel", "parallel"), vmem),
        name="diff_attention" if diff else "gqa_attention",
    )(qk, qk, p, qk, p, aux)


def _conv_kernel(cur_ref, prev_ref, next_ref, w_ref, o_ref, z_scr, u_scr, *, tiles_lat, tiles_ctx, n_lat_tiles):
    t = pl.program_id(0)
    in_lat = t < n_lat_tiles
    pos = jnp.where(in_lat, t % tiles_lat, (t - n_lat_tiles) % tiles_ctx)
    last = jnp.where(in_lat, tiles_lat, tiles_ctx) - 1
    has_prev = (pos > 0).astype(F32)
    has_next = (pos < last).astype(F32)
    T, H = cur_ref.shape[0], CONV_HALO
    W = 512

    def cols(ref, k):
        return ref[:, k * W:(k + 1) * W].astype(F32)

    def glu(ref):
        return cols(ref, 3) * (1.0 / (1.0 + jnp.exp(-cols(ref, 4))))

    z_scr[0:H, :] = cols(prev_ref, 1) * cols(prev_ref, 2) * has_prev
    z_scr[H:H + T, :] = cols(cur_ref, 1) * cols(cur_ref, 2)
    z_scr[H + T:, :] = cols(next_ref, 1) * cols(next_ref, 2) * has_next
    acc = jnp.zeros((T, W), F32)
    for k in range(SC_K):
        off = H + k - SC_K // 2
        acc = acc + w_ref[k:k + 1, :] * z_scr[off:off + T, :]
    o_ref[:, :W] = (cols(cur_ref, 0) * acc).astype(o_ref.dtype)

    u_scr[0:H, :] = glu(prev_ref) * has_prev
    u_scr[H:H + T, :] = glu(cur_ref)
    u_scr[H + T:, :] = glu(next_ref) * has_next
    acc = jnp.zeros((T, W), F32)
    for k in range(CF_K):
        off = H + k - CF_K // 2
        acc = acc + w_ref[SC_K + k:SC_K + k + 1, :] * u_scr[off:off + T, :]
    r = SC_K + CF_K
    u = acc + w_ref[r:r + 1, :]
    mu = jnp.mean(u, axis=-1, keepdims=True)
    var = jnp.mean(jnp.square(u - mu), axis=-1, keepdims=True)
    y = (u - mu) * lax.rsqrt(var + EPS) * w_ref[r + 1:r + 2, :] + w_ref[r + 2:r + 3, :]
    o_ref[:, W:] = (y * (1.0 / (1.0 + jnp.exp(-y)))).astype(o_ref.dtype)


def _conv_mixers(p, conv_w, *, n_rows, n_lat_rows, seq, ctx_len):
    T, H = CONV_TILE, CONV_HALO
    r = T // H
    n_halo_blocks = p.shape[0] // H
    cblk = CONV_COL0 // CONV_COLS
    kernel = functools.partial(_conv_kernel, tiles_lat=seq // T, tiles_ctx=ctx_len // T, n_lat_tiles=n_lat_rows // T)
    return pl.pallas_call(
        kernel,
        grid=(n_rows // T,),
        in_specs=[pl.BlockSpec((T, CONV_COLS), lambda t: (t, cblk)),
                  pl.BlockSpec((H, CONV_COLS), lambda t: (jnp.maximum(t * r - 1, 0), cblk)),
                  pl.BlockSpec((H, CONV_COLS), lambda t: (jnp.minimum((t + 1) * r, n_halo_blocks - 1), cblk)),
                  pl.BlockSpec(conv_w.shape, lambda t: (0, 0))],
        out_specs=pl.BlockSpec((T, 1024), lambda t: (t, 0)),
        out_shape=jax.ShapeDtypeStruct((n_rows, 1024), BF16),
        scratch_shapes=[pltpu.VMEM((T + 2 * H, 512), F32), pltpu.VMEM((T + 2 * H, 512), F32)],
        compiler_params=_cparams(("parallel",), 4 * T * CONV_COLS * 2 + 16 * T * 512 * 4),
        name="conv_mixers",
    )(p, p, p, conv_w)


def _merge_kernel(ya_ref, yb_ref, yc_ref, g_ref, x_ref, mod_ref, wbr_ref, wo_ref, o_ref):
    D = x_ref.shape[1]
    W = 512
    ys = (ya_ref[...], yb_ref[...], yc_ref[:, :W], yc_ref[:, W:])
    merged = jnp.zeros(x_ref.shape, F32)
    for i in range(N_BRANCH):
        z = jnp.dot(ys[i], wbr_ref[i], preferred_element_type=F32)
        gate = g_ref[:, i * D:(i + 1) * D].astype(F32)
        merged = merged + z * (1.0 / (1.0 + jnp.exp(-gate)))
    out = jnp.dot(merged.astype(BF16), wo_ref[...], preferred_element_type=F32)
    o_ref[...] = x_ref[...] + mod_ref[0, 2:3, :] * out


def _merge(ya, yb, yc, gates, x, mod_l, wbr, wo, *, n_rows, n_lat_rows, seq):
    D = x.shape[1]
    nb = mod_l.shape[0] - 1
    tm = CONV_TILE
    grp = functools.partial(_group_of, tile_rows=tm, n_lat_rows=n_lat_rows, seq=seq, n_batch=nb)
    vmem = (2 * tm * N_BRANCH * D * 2 + 4 * tm * D * 4 + N_BRANCH * 512 * D * 2 + D * D * 2
            + 4 * tm * D * 4 + 2 * tm * 2048 * 2)
    return pl.pallas_call(
        _merge_kernel,
        grid=(n_rows // tm,),
        in_specs=[pl.BlockSpec((tm, 512), lambda i: (i, 0)),
                  pl.BlockSpec((tm, 512), lambda i: (i, 0)),
                  pl.BlockSpec((tm, 1024), lambda i: (i, 0)),
                  pl.BlockSpec((tm, N_BRANCH * D), lambda i: (i, 0)),
                  pl.BlockSpec((tm, D), lambda i: (i, 0)),
                  pl.BlockSpec((1, N_MOD, D), lambda i: (grp(i), 0, 0)),
                  pl.BlockSpec(wbr.shape, lambda i: (0, 0, 0), pipeline_mode=pl.Buffered(1)),
                  pl.BlockSpec(wo.shape, lambda i: (0, 0), pipeline_mode=pl.Buffered(1))],
        out_specs=pl.BlockSpec((tm, D), lambda i: (i, 0)),
        out_shape=jax.ShapeDtypeStruct((n_rows, D), F32),
        compiler_params=_cparams(("parallel",), vmem),
        name="merge_branches",
    )(ya, yb, yc, gates, x, mod_l, wbr, wo)


def _oddeven_merge_sort_pairs(n):
    pairs = []

    def merge(lo, hi, r):
        step = r * 2
        if step < hi - lo:
            merge(lo, hi, step)
            merge(lo + r, hi, step)
            pairs.extend((i, i + r) for i in range(lo + r, hi - r, step))
        else:
            pairs.append((lo, lo + r))

    def sort(lo, hi):
        if hi - lo >= 1:
            mid = lo + (hi - lo) // 2
            sort(lo, mid)
            sort(mid + 1, hi)
            merge(lo, hi, 1)

    sort(0, n - 1)
    return pairs


def _top_values(work):
    K, SUB = PK_TOPK, 8
    T = work.shape[1]
    assert work.shape[0] % SUB == 0 and work.shape[0] <= K * SUB
    slabs = [work[SUB * i:SUB * (i + 1)] for i in range(work.shape[0] // SUB)]
    slabs += [jnp.full((SUB, T), -jnp.inf, F32)] * (K - len(slabs))

    def exchange(i, j):
        slabs[i], slabs[j] = jnp.maximum(slabs[i], slabs[j]), jnp.minimum(slabs[i], slabs[j])

    for i, j in _oddeven_merge_sort_pairs(K):
        exchange(i, j)
    for shift in (4, 2, 1):
        partner = [pltpu.roll(s, shift, 0) for s in slabs]
        slabs = [jnp.maximum(slabs[i], partner[K - 1 - i]) for i in range(K)]
        for dist in (8, 4, 2, 1):
            for i in range(K):
                if i & dist == 0:
                    exchange(i, i + dist)
    return [s[0:1] for s in slabs]


def _stack_rows(rows):
    n, T = len(rows), rows[0].shape[1]
    idx = lax.broadcasted_iota(jnp.int32, (n, T), 0)
    out = jnp.zeros((n, T), F32)
    for r, row in enumerate(rows):
        out = jnp.where(idx == r, row, out)
    return out


def _peer_select_kernel(q_ref, k1_ref, k2_ref, s1_ref, s2_ref, b_ref, c_ref, tau_ref):
    K = PK_TOPK
    nt = (((1,), (1,)), ((), ()))
    s1 = lax.dot_general(k1_ref[...], q_ref[:, :N_KEYS], nt, preferred_element_type=F32)
    s2 = lax.dot_general(k2_ref[...], q_ref[:, N_KEYS:], nt, preferred_element_type=F32)
    t1 = _top_values(s1)
    t2 = _top_values(s2)
    T1, T2 = _stack_rows(t1), _stack_rows(t2)
    ninf = jnp.full((8, s1.shape[1]), -jnp.inf, F32)
    r8 = lax.broadcasted_iota(jnp.int32, (8, s1.shape[1]), 0)
    slabs = [t1[0] + T2[0:8], t1[0] + T2[8:16], t1[1] + T2[0:8]]
    slabs.append(jnp.where(r8 < K // 3, t1[2] + T2[0:8], ninf))
    slabs.append(jnp.where(r8 < K // 4, t1[3] + T2[0:8], ninf))
    slabs.append(T1[8:16] + t2[0])
    slabs.append(jnp.where(r8 >= 4, T1[0:8] + t2[0], ninf))
    slabs.append(jnp.where(r8 >= 4, T1[0:8] + t2[1], ninf))
    slabs.append(jnp.where(r8 == 4, T1[0:8] + t2[2], ninf))
    cand = jnp.concatenate(slabs, axis=0)
    best = _top_values(cand)
    top, tau = best[0], best[K - 1]
    z = jnp.sum(jnp.where(cand >= tau, jnp.exp(cand - top), 0.0), axis=0, keepdims=True)
    s1_ref[0] = s1
    s2_ref[0] = s2
    b_ref[0] = jnp.exp(s1 - t1[0]) / z
    c_ref[0] = jnp.exp(s2 - t2[0])
    tau_ref[0] = tau


def _peer_select(q, k1, k2):
    n = q.shape[0]
    tm = ROW_TILE
    big = jax.ShapeDtypeStruct((PK_HEADS, N_KEYS, n), F32)
    big_spec = pl.BlockSpec((1, N_KEYS, tm), lambda i, h: (h, 0, i))
    return pl.pallas_call(
        _peer_select_kernel,
        grid=(n // tm, PK_HEADS),
        in_specs=[pl.BlockSpec((tm, 2 * N_KEYS), lambda i, h: (i, h)),
                  pl.BlockSpec(k1.shape, lambda i, h: (0, 0)),
                  pl.BlockSpec(k2.shape, lambda i, h: (0, 0))],
        out_specs=[big_spec, big_spec, big_spec, big_spec, pl.BlockSpec((1, 1, tm), lambda i, h: (h, 0, i))],
        out_shape=[big, big, big, big, jax.ShapeDtypeStruct((PK_HEADS, 1, n), F32)],
        compiler_params=_cparams(("parallel", "parallel"), 8 * N_KEYS * tm * 4 + 32 * N_KEYS * tm * 4),
        name="peer_select",
    )(q, k1, k2)


def _gelu_tanh(x):
    return 0.5 * x * (1.0 + jnp.tanh(math.sqrt(2.0 / math.pi) * (x + 0.044715 * (x * x * x))))


def _peer_dense_kernel(h_ref, u_ref, vt_ref, s1r_ref, br_ref, s2_ref, c_ref, tau_ref, x_ref, mod_ref, o_ref,
                       acc_scr, s0_scr, s1_scr, w0_scr, w1_scr, *, nj):
    t = pl.program_id(0)
    tm = h_ref.shape[0]
    jc = jnp.maximum(t - 2, 0) % nj

    @pl.when(t == 0)
    def _():
        for ref in (s0_scr, s1_scr, w0_scr, w1_scr):
            ref[...] = jnp.zeros(ref.shape, ref.dtype)

    @pl.when(jc == 0)
    def _():
        acc_scr[...] = jnp.zeros(acc_scr.shape, F32)

    def tick(s_write, s_read, w_write, w_read):
        n_slabs = u_ref.shape[0] // N_KEYS
        n_key_blocks = 8
        kb = N_KEYS // n_key_blocks

        def stage_b(key_block, lc):
            keys = slice(key_block * kb, (key_block + 1) * kb)
            cols = slice(lc * LANES, (lc + 1) * LANES)
            gates = [None] * n_slabs
            for hd in range(PK_HEADS):
                s2, cf, tau = s2_ref[hd, keys, cols], c_ref[hd, keys, cols], tau_ref[hd, :, cols]
                for b in range(n_slabs):
                    total = s2 + s1r_ref[b, hd:hd + 1, cols]
                    term = jnp.where(total >= tau, cf * br_ref[b, hd:hd + 1, cols], 0.0)
                    gates[b] = term if gates[b] is None else gates[b] + term
            for b in range(n_slabs):
                rows = slice(b * N_KEYS + keys.start, b * N_KEYS + keys.stop)
                w_write[rows, cols] = (gates[b] * _gelu_tanh(s_read[rows, cols])).astype(BF16)

        nt = (((1,), (1,)), ((), ()))
        kp_size = 256
        n_a_pieces = u_ref.shape[1] // kp_size
        n_c_pieces = vt_ref.shape[2] // kp_size
        n_c_blocks = 8
        c_rows = acc_scr.shape[0] // n_c_blocks
        units = [(key_block, lc) for key_block in range(n_key_blocks) for lc in range(tm // LANES)]
        n_groups = n_c_blocks * n_c_pieces
        a_per_group = 2 * n_a_pieces // n_groups
        assert a_per_group * n_groups == 2 * n_a_pieces
        a_acc, c_acc = None, None
        for g in range(n_groups):
            for i in range(a_per_group):
                half, piece = divmod(g * a_per_group + i, n_a_pieces)
                cols = slice(half * (tm // 2), (half + 1) * (tm // 2))
                ks = slice(piece * kp_size, (piece + 1) * kp_size)
                d = lax.dot_general(u_ref[:, ks], h_ref[cols, ks], nt, preferred_element_type=F32)
                a_acc = d if piece == 0 else a_acc + d
                if piece == n_a_pieces - 1:
                    s_write[:, cols] = a_acc
            blk, piece = divmod(g, n_c_pieces)
            rows = slice(blk * c_rows, (blk + 1) * c_rows)
            ks = slice(piece * kp_size, (piece + 1) * kp_size)
            if piece == 0:
                c_acc = acc_scr[rows, :]
            c_acc = c_acc + jnp.dot(vt_ref[0, rows, ks], w_read[ks, :], preferred_element_type=F32)
            if piece == n_c_pieces - 1:
                acc_scr[rows, :] = c_acc
            for _ in range(len(units) // (n_groups - g)):
                stage_b(*units.pop(0))

    parity = t % 2
    pl.when(parity == 0)(functools.partial(tick, s0_scr, s1_scr, w1_scr, w0_scr))
    pl.when(parity == 1)(functools.partial(tick, s1_scr, s0_scr, w0_scr, w1_scr))

    @pl.when(jnp.logical_and(jc == nj - 1, t >= 2))
    def _():
        o_ref[...] = x_ref[...] + mod_ref[0, 5:6, :] * acc_scr[...].T


def _peer_dense(h, u, vt, s1r, br, s2, c, tau, x, mod_l, *, n_rows, n_lat_rows, seq):
    D = x.shape[1]
    n_exp = u.shape[0]
    nb = mod_l.shape[0] - 1
    tm, te = ROW_TILE, PEER_TE
    nk = te // N_KEYS
    nj = n_exp // te
    n_items = (n_rows // tm) * nj
    grp = functools.partial(_group_of, tile_rows=tm, n_lat_rows=n_lat_rows, seq=seq, n_batch=nb)

    def item(t, lag):
        k = jnp.clip(t - lag, 0, n_items - 1)
        return k // nj, k % nj

    sel_spec = pl.BlockSpec((PK_HEADS, N_KEYS, tm), lambda t: (0, 0, item(t, 1)[0]))
    row_spec = pl.BlockSpec((nk, PK_HEADS, tm), lambda t: (item(t, 1)[1], 0, item(t, 1)[0]))
    vmem = (2 * tm * D * 2 + 4 * te * D * 2 + 4 * PK_HEADS * N_KEYS * tm * 4 + 4 * tm * D * 4
            + D * tm * 4 + 2 * te * tm * 6 + 2 * D * tm * 4)
    return pl.pallas_call(
        functools.partial(_peer_dense_kernel, nj=nj),
        grid=(n_items + 2,),
        in_specs=[pl.BlockSpec((tm, D), lambda t: (item(t, 0)[0], 0)),
                  pl.BlockSpec((te, D), lambda t: (item(t, 0)[1], 0)),
                  pl.BlockSpec((1, D, te), lambda t: (item(t, 2)[1], 0, 0)),
                  row_spec, row_spec, sel_spec, sel_spec,
                  pl.BlockSpec((PK_HEADS, 1, tm), lambda t: (0, 0, item(t, 1)[0])),
                  pl.BlockSpec((tm, D), lambda t: (item(t, 2)[0], 0)),
                  pl.BlockSpec((1, N_MOD, D), lambda t: (grp(item(t, 2)[0]), 0, 0))],
        out_specs=pl.BlockSpec((tm, D), lambda t: (item(t, 2)[0], 0)),
        out_shape=jax.ShapeDtypeStruct((n_rows, D), F32),
        scratch_shapes=[pltpu.VMEM((D, tm), F32), pltpu.VMEM((te, tm), F32), pltpu.VMEM((te, tm), F32),
                        pltpu.VMEM((te, tm), BF16), pltpu.VMEM((te, tm), BF16)],
        compiler_params=_cparams(("arbitrary",), vmem),
        name="peer_dense",
    )(h, u, vt, s1r, br, s2, c, tau, x, mod_l)


def _rope_tables(seq, pad_rows):
    t = jnp.arange(seq, dtype=jnp.int32)
    row = (t // GRID_W).astype(F32)
    col = (t % GRID_W).astype(F32)

    def tables(dim, lanes_per_pair):
        n_pair = dim // 4
        inv = ROPE_THETA ** (-jnp.arange(n_pair, dtype=F32) / n_pair)
        ang = jnp.concatenate([row[:, None] * inv, col[:, None] * inv], axis=-1)
        lane_pair = np.arange(LANES) % lanes_per_pair
        sign = np.where(np.arange(LANES) < 64, -1.0, 1.0).astype(np.float32)
        return jnp.cos(ang)[:, lane_pair], jnp.sin(ang)[:, lane_pair] * sign

    ca, sa = tables(64, 32)
    cb, sb = tables(128, 64)
    tab = jnp.stack([ca, sa, cb, sb])
    ident = jnp.stack([jnp.ones((pad_rows, LANES), F32), jnp.zeros((pad_rows, LANES), F32)] * 2)
    return jnp.concatenate([tab, ident], axis=1)


def _permuted_proj_weight(w):
    D = w.shape[0]

    def da(blk):
        return blk.reshape(D, DA_HEADS, 2, 32, 2).transpose(0, 1, 4, 2, 3).reshape(D, DA_HEADS * LANES)

    def gq(blk):
        return blk.reshape(D, -1, 64, 2).transpose(0, 1, 3, 2).reshape(D, blk.shape[1])

    return jnp.concatenate([da(w[:, 0:512]), da(w[:, 512:1024]), gq(w[:, 1536:2048]), gq(w[:, 2048:2304]),
                            w[:, 1024:1536], w[:, 2304:5120]], axis=1)


def _qk_gains(da_qn_g, da_kn_g, gq_qn_g, gq_kn_g):
    da, gq = _da_lane_perm() % 64, _gq_lane_perm()
    log2e = math.log2(math.e)
    rows = ([da_qn_g[da] * (64 ** -0.5 * log2e)] * DA_HEADS + [da_kn_g[da]] * DA_HEADS
            + [gq_qn_g[gq] * (128 ** -0.5 * log2e)] * GQ_HEADS + [gq_kn_g[gq]] * GQ_KV)
    return jnp.stack(rows).astype(F32)


def kernel(x, c, ctx, c_ctx, mod_w, mod_b, norm1_g, norm2_g, w_in, da_qn_g, da_kn_g, da_lam, da_head_g,
           gq_qn_g, gq_kn_g, sc_w, cf_w, cf_b, cf_ln_g, cf_ln_b, w_branch, w_out,
           pk_wq, pk_k1, pk_k2, pk_u, pk_v):
    B, S, D = x.shape
    C = ctx.shape[1]
    L = mod_w.shape[0]
    n_lat, n_all = B * S, B * S + B * C
    assert S % ROW_TILE == 0 and (B * C) % ROW_TILE == 0 and C % CONV_TILE == 0 and S % min(ATT_TK, S) == 0
    assert B + 1 <= 8 and D == N_BRANCH * 512 and w_in.shape[2] == 5120 + N_BRANCH * D

    xs = jnp.concatenate([x.reshape(n_lat, D), ctx.reshape(B * C, D)], axis=0)
    cs = jnp.zeros((8, D), F32).at[:B].set(c).at[B].set(c_ctx)
    mod = _modulation(cs, mod_w, mod_b).reshape(L, 8, N_MOD, D)[:, :B + 1]
    rope_tab = _rope_tables(S, ROW_TILE)
    geom = dict(n_lat_rows=n_lat, seq=S)

    for l in range(L):
        with_ctx = l < L - 1
        n_rows = n_all if with_ctx else n_lat
        lam_init = 0.8 - 0.6 * math.exp(-0.3 * l)
        lp = da_lam[l].astype(F32)
        lam = jnp.exp(jnp.sum(lp[0] * lp[1])) - jnp.exp(jnp.sum(lp[2] * lp[3])) + lam_init
        aux = jnp.zeros((8, LANES), F32).at[0].set(lam).at[1].set(da_head_g[l] * (1.0 - lam_init))
        w_p = _permuted_proj_weight(w_in[l][:, :5120]).astype(BF16)
        w_g = w_in[l][:, 5120:].astype(BF16)
        conv_w = jnp.concatenate([sc_w[l], cf_w[l], cf_b[l][None], cf_ln_g[l][None], cf_ln_b[l][None],
                                  jnp.zeros((3, 512), F32)], axis=0)
        mod_l = mod[l]

        p = _norm_mod_matmul(xs, norm1_g[l], mod_l, w_p, shift=0, scale=1, n_rows=n_all, emit_h=False,
                             name="in_proj", **geom)
        gates = _norm_mod_matmul(xs, norm1_g[l], mod_l, w_g, shift=0, scale=1, n_rows=n_rows, emit_h=False,
                                 name="gate_proj", **geom)
        qk = _qk_prep(p, rope_tab, _qk_gains(da_qn_g[l], da_kn_g[l], gq_qn_g[l], gq_kn_g[l]), **geom)
        att = dict(n_batch=B, seq=S, ctx_len=C, with_ctx_queries=with_ctx)
        ya = _attention(qk, p, aux, diff=True, **att)
        yb = _attention(qk, p, aux, diff=False, **att)
        yc = _conv_mixers(p, conv_w, n_rows=n_rows, ctx_len=C, **geom)
        xs = _merge(ya, yb, yc, gates, xs, mod_l, w_branch[l].astype(BF16), w_out[l].astype(BF16),
                    n_rows=n_rows, **geom)

        q, h2 = _norm_mod_matmul(xs, norm2_g[l], mod_l, pk_wq[l].astype(BF16), shift=3, scale=4, n_rows=n_rows,
                                 emit_h=True, name="peer_query", **geom)
        s1, s2, bf, cf, tau = _peer_select(q, pk_k1[l].astype(BF16), pk_k2[l].astype(BF16))
        s1r, br = jnp.transpose(s1, (1, 0, 2)), jnp.transpose(bf, (1, 0, 2))
        vt = pk_v[l].reshape(-1, PEER_TE, D).transpose(0, 2, 1).astype(BF16)
        xs = _peer_dense(h2, pk_u[l].astype(BF16), vt, s1r, br, s2, cf, tau, xs, mod_l,
                         n_rows=n_rows, **geom)
    return xs[:n_lat].reshape(B, S, D)
```

```python
import functools
import math

import numpy as np
import jax
import jax.numpy as jnp
from jax import lax
from jax.experimental import pallas as pl
from jax.experimental.pallas import tpu as pltpu

F32 = jnp.float32
BF16 = jnp.bfloat16

EPS = 1e-6
N_MOD = 6
GRID_W = 64
ROPE_THETA = 10000.0
N_BRANCH = 4
DA_HEADS = 4
GQ_HEADS = 4
GQ_KV = 2
SC_K = 3
CF_K = 31
PK_HEADS = 8
N_KEYS = 128
PK_TOPK = 16

LANES = 128
V7X_VMEM_BYTES = 64 * 1024 * 1024
VMEM_CAP = V7X_VMEM_BYTES - 8 * 1024 * 1024

ROW_TILE = 512
CONV_TILE = 256
CONV_HALO = 16
ATT_TQ = 256
ATT_TK = 1024
PEER_TE = 512

QK_COLS = (2 * DA_HEADS + GQ_HEADS + GQ_KV) * LANES
V_DA_BLK = QK_COLS // LANES
V_GQ_BLK = V_DA_BLK + DA_HEADS
CONV_COL0 = (V_GQ_BLK + GQ_KV) * LANES
CONV_COLS = 5 * 512
P_COLS = CONV_COL0 + CONV_COLS


def _cparams(sem, vmem_bytes):
    limit = int(min(max(vmem_bytes * 5 // 4 + (4 << 20), 16 << 20), VMEM_CAP))
    return pltpu.CompilerParams(dimension_semantics=sem, vmem_limit_bytes=limit)


def _group_of(tile, tile_rows, n_lat_rows, seq, n_batch):
    n_lat_tiles = n_lat_rows // tile_rows
    return jnp.where(tile >= n_lat_tiles, n_batch, tile // (seq // tile_rows))


def _mod_kernel(c_ref, w_ref, b_ref, o_ref):
    c = c_ref[...]
    sc = c * (1.0 / (1.0 + jnp.exp(-c)))
    o_ref[0] = jnp.dot(sc.astype(BF16), w_ref[0].astype(BF16), preferred_element_type=F32) + b_ref[0]


def _modulation(cs, mod_w, mod_b):
    L, D, W = mod_w.shape
    tn = 1024
    return pl.pallas_call(
        _mod_kernel,
        grid=(L, W // tn),
        in_specs=[pl.BlockSpec((8, D), lambda l, j: (0, 0)),
                  pl.BlockSpec((1, D, tn), lambda l, j: (l, 0, j)),
                  pl.BlockSpec((1, 1, tn), lambda l, j: (l, 0, j))],
        out_specs=pl.BlockSpec((1, 8, tn), lambda l, j: (l, 0, j)),
        out_shape=jax.ShapeDtypeStruct((L, 8, W), F32),
        compiler_params=_cparams(("parallel", "parallel"), 2 * D * tn * 4 + D * tn * 2),
        name="modulation",
    )(cs, mod_w, mod_b.reshape(L, 1, W))


def _nmm_kernel(x_ref, g_ref, mod_ref, w_ref, *rest, shift, scale, emit_h):
    if emit_h:
        o_ref, h_ref, h_scr = rest
    else:
        o_ref, h_scr = rest

    @pl.when(pl.program_id(1) == 0)
    def _():
        x = x_ref[...]
        ms = jnp.mean(x * x, axis=-1, keepdims=True)
        y = x * lax.rsqrt(ms + EPS) * g_ref[...]
        h = y * (1.0 + mod_ref[0, scale:scale + 1, :]) + mod_ref[0, shift:shift + 1, :]
        h_scr[...] = h.astype(BF16)
        if emit_h:
            h_ref[...] = h_scr[...]

    o_ref[...] = jnp.dot(h_scr[...], w_ref[...], preferred_element_type=F32).astype(o_ref.dtype)


def _norm_mod_matmul(x, g, mod_l, w, *, shift, scale, n_rows, n_lat_rows, seq, emit_h, name):
    D = x.shape[1]
    cols = w.shape[1]
    nb = mod_l.shape[0] - 1
    tm, tn = ROW_TILE, (1024 if cols % 1024 == 0 else 512)
    grp = functools.partial(_group_of, tile_rows=tm, n_lat_rows=n_lat_rows, seq=seq, n_batch=nb)
    out_shape = [jax.ShapeDtypeStruct((n_rows, cols), BF16)]
    out_specs = [pl.BlockSpec((tm, tn), lambda i, j: (i, j))]
    if emit_h:
        out_shape.append(jax.ShapeDtypeStruct((n_rows, D), BF16))
        out_specs.append(pl.BlockSpec((tm, D), lambda i, j: (i, 0)))
    vmem = 2 * tm * D * 4 + 2 * D * tn * 2 + 2 * tm * tn * 2 + 3 * tm * D * 2 + 2 * tm * D * 4
    res = pl.pallas_call(
        functools.partial(_nmm_kernel, shift=shift, scale=scale, emit_h=emit_h),
        grid=(n_rows // tm, cols // tn),
        in_specs=[pl.BlockSpec((tm, D), lambda i, j: (i, 0)),
                  pl.BlockSpec((1, D), lambda i, j: (0, 0)),
                  pl.BlockSpec((1, N_MOD, D), lambda i, j: (grp(i), 0, 0)),
                  pl.BlockSpec((D, tn), lambda i, j: (0, j))],
        out_specs=out_specs,
        out_shape=out_shape,
        scratch_shapes=[pltpu.VMEM((tm, D), BF16)],
        compiler_params=_cparams(("parallel", "arbitrary"), vmem),
        name=name,
    )(x, g.reshape(1, D), mod_l, w)
    return res if emit_h else res[0]


def _da_lane_perm():
    j = np.arange(LANES)
    comp, parity, pair = (j // 32) % 2, j // 64, j % 32
    return comp * 64 + 2 * pair + parity


def _gq_lane_perm():
    j = np.arange(LANES)
    return 2 * (j % 64) + j // 64


def _qk_prep_kernel(p_ref, tab_ref, gain_ref, o_ref):
    lane = lax.broadcasted_iota(jnp.int32, (1, LANES), 1)
    comp0 = (lane % 64) < 32
    n_da = 2 * DA_HEADS
    for hd in range(QK_COLS // LANES):
        sl = slice(hd * LANES, (hd + 1) * LANES)
        x = p_ref[:, sl].astype(F32)
        xx = x * x
        tot = jnp.sum(xx, axis=-1, keepdims=True)
        if hd < n_da:
            s0 = jnp.sum(jnp.where(comp0, xx, 0.0), axis=-1, keepdims=True)
            ms = jnp.where(comp0, s0, tot - s0) * (1.0 / 64.0)
            cos, sin = tab_ref[0], tab_ref[1]
        else:
            ms = tot * (1.0 / 128.0)
            cos, sin = tab_ref[2], tab_ref[3]
        y = x * lax.rsqrt(ms + EPS) * gain_ref[hd:hd + 1, :]
        o_ref[:, sl] = (y * cos + pltpu.roll(y, 64, 1) * sin).astype(BF16)


def _qk_prep(p, rope_tab, gains, *, n_lat_rows, seq):
    n = p.shape[0]
    tm = ROW_TILE
    n_lat_tiles = n_lat_rows // tm
    rope_blk = lambda i: jnp.where(i >= n_lat_tiles, seq // tm, i % (seq // tm))
    return pl.pallas_call(
        _qk_prep_kernel,
        grid=(n // tm,),
        in_specs=[pl.BlockSpec((tm, QK_COLS), lambda i: (i, 0)),
                  pl.BlockSpec((4, tm, LANES), lambda i: (0, rope_blk(i), 0)),
                  pl.BlockSpec((QK_COLS // LANES, LANES), lambda i: (0, 0))],
        out_specs=pl.BlockSpec((tm, QK_COLS), lambda i: (i, 0)),
        out_shape=jax.ShapeDtypeStruct((n, QK_COLS), BF16),
        compiler_params=_cparams(("parallel",), 4 * tm * QK_COLS * 2 + 8 * tm * LANES * 4 + 6 * tm * LANES * 4),
        name="qk_prep",
    )(p, rope_tab, gains)


def _flash_kernel(q_ref, kl_ref, vl_ref, kc_ref, vc_ref, aux_ref, o_ref, *, diff, n_lat_chunks, nq_lat, tk,
                  ctx_queries):
    tq = q_ref.shape[0]
    lane = lax.broadcasted_iota(jnp.int32, (1, LANES), 1)
    if diff:
        q = q_ref[...]
        comp0 = (lane % 64) < 32
        zero = jnp.zeros_like(q)
        q2 = jnp.concatenate([jnp.where(comp0, q, zero), jnp.where(comp0, zero, q)], axis=0)
    else:
        q2 = jnp.concatenate([q_ref[:, :LANES], q_ref[:, LANES:]], axis=0)

    def scores(k):
        return lax.dot_general(q2, k, (((1,), (1,)), ((), ())), preferred_element_type=F32)

    def weighted(p, v):
        ones_col = jnp.broadcast_to(jnp.where(lane == 0, 1.0, 0.0).astype(BF16), v.shape)
        return jnp.dot(p.astype(BF16), jnp.concatenate([v, ones_col], axis=1), preferred_element_type=F32)

    def fold(carry, parts):
        m_new = functools.reduce(jnp.maximum, [jnp.max(s, axis=-1, keepdims=True) for s, _ in parts])
        if carry is not None:
            m_new = jnp.maximum(carry[0], m_new)
        acc = None
        for s, v in parts:
            pv = weighted(jnp.exp2(s - m_new), v)
            acc = pv if acc is None else acc + pv
        if carry is not None:
            acc = acc + jnp.exp2(carry[0] - m_new) * carry[1]
        return m_new, acc

    def finish(carry):
        acc = carry[1]
        o = acc[:, :LANES] / acc[:, LANES:LANES + 1]
        if diff:
            d = o[:tq] - aux_ref[0:1, :] * o[tq:]
            ms = jnp.mean(d * d, axis=-1, keepdims=True)
            o_ref[...] = (d * lax.rsqrt(ms + EPS) * aux_ref[1:2, :]).astype(o_ref.dtype)
        else:
            o_ref[:, :LANES] = o[:tq].astype(o_ref.dtype)
            o_ref[:, LANES:] = o[tq:].astype(o_ref.dtype)

    def latent_queries():
        carry = None
        for c in range(n_lat_chunks):
            parts = [(scores(kl_ref[c * tk:(c + 1) * tk, :]), vl_ref[c * tk:(c + 1) * tk, :])]
            if c == n_lat_chunks - 1:
                parts.append((scores(kc_ref[...]), vc_ref[...]))
            carry = fold(carry, parts)
        finish(carry)

    if ctx_queries:
        is_lat = pl.program_id(2) < nq_lat
        pl.when(is_lat)(latent_queries)

        @pl.when(jnp.logical_not(is_lat))
        def _():
            finish(fold(None, [(scores(kc_ref[...]), vc_ref[...])]))
    else:
        latent_queries()


def _attention(qk, p, aux, *, diff, n_batch, seq, ctx_len, with_ctx_queries):
    tq, tk = ATT_TQ, min(ATT_TK, seq)
    nq_lat, nq_ctx = seq // tq, ctx_len // tq
    n_lat_rows = n_batch * seq
    if diff:
        n_heads, qw = DA_HEADS, LANES
        q_blk = lambda h: h
        k_blk = lambda h: DA_HEADS + h
        v_blk = lambda h: V_DA_BLK + h
    else:
        n_heads, qw = GQ_KV, 2 * LANES
        q_blk = lambda h: (2 * DA_HEADS * LANES) // qw + h
        k_blk = lambda h: 2 * DA_HEADS + GQ_HEADS + h
        v_blk = lambda h: V_GQ_BLK + h
    nq = nq_lat + (nq_ctx if with_ctx_queries else 0)
    q_row = lambda b, qi: jnp.where(qi >= nq_lat, n_batch * nq_lat + b * nq_ctx + qi - nq_lat, b * nq_lat + qi)
    ctx_row = lambda b: n_lat_rows // ctx_len + b
    kernel = functools.partial(_flash_kernel, diff=diff, n_lat_chunks=seq // tk, nq_lat=nq_lat, tk=tk,
                               ctx_queries=with_ctx_queries)
    n_out = n_lat_rows + (n_batch * ctx_len if with_ctx_queries else 0)
    vmem = 4 * seq * LANES * 2 + 2 * tq * (tk + ctx_len) * 4 * 4 + 8 * tq * LANES * 4 * 3
    return pl.pallas_call(
        kernel,
        grid=(n_batch, n_heads, nq),
        in_specs=[pl.BlockSpec((tq, qw), lambda b, h, qi: (q_row(b, qi), q_blk(h))),
                  pl.BlockSpec((seq, LANES), lambda b, h, qi: (b, k_blk(h))),
                  pl.BlockSpec((seq, LANES), lambda b, h, qi: (b, v_blk(h))),
                  pl.BlockSpec((ctx_len, LANES), lambda b, h, qi: (ctx_row(b), k_blk(h))),
                  pl.BlockSpec((ctx_len, LANES), lambda b, h, qi: (ctx_row(b), v_blk(h))),
                  pl.BlockSpec((8, LANES), lambda b, h, qi: (0, 0))],
        out_specs=pl.BlockSpec((tq, qw), lambda b, h, qi: (q_row(b, qi), h)),
        out_shape=jax.ShapeDtypeStruct((n_out, n_heads * qw), BF16),
        compiler_params=_cparams(("parallel", "parallel", "parallel"), vmem),
        name="diff_attention" if diff else "gqa_attention",
    )(qk, qk, p, qk, p, aux)


def _conv_kernel(cur_ref, prev_ref, next_ref, w_ref, o_ref, z_scr, u_scr, *, tiles_lat, tiles_ctx, n_lat_tiles):
    t = pl.program_id(0)
    in_lat = t < n_lat_tiles
    pos = jnp.where(in_lat, t % tiles_lat, (t - n_lat_tiles) % tiles_ctx)
    last = jnp.where(in_lat, tiles_lat, tiles_ctx) - 1
    has_prev = (pos > 0).astype(F32)
    has_next = (pos < last).astype(F32)
    T, H = cur_ref.shape[0], CONV_HALO
    W = 512

    def cols(ref, k):
        return ref[:, k * W:(k + 1) * W].astype(F32)

    def glu(ref):
        return cols(ref, 3) * (1.0 / (1.0 + jnp.exp(-cols(ref, 4))))

    z_scr[0:H, :] = cols(prev_ref, 1) * cols(prev_ref, 2) * has_prev
    z_scr[H:H + T, :] = cols(cur_ref, 1) * cols(cur_ref, 2)
    z_scr[H + T:, :] = cols(next_ref, 1) * cols(next_ref, 2) * has_next
    acc = jnp.zeros((T, W), F32)
    for k in range(SC_K):
        off = H + k - SC_K // 2
        acc = acc + w_ref[k:k + 1, :] * z_scr[off:off + T, :]
    o_ref[:, :W] = (cols(cur_ref, 0) * acc).astype(o_ref.dtype)

    u_scr[0:H, :] = glu(prev_ref) * has_prev
    u_scr[H:H + T, :] = glu(cur_ref)
    u_scr[H + T:, :] = glu(next_ref) * has_next
    acc = jnp.zeros((T, W), F32)
    for k in range(CF_K):
        off = H + k - CF_K // 2
        acc = acc + w_ref[SC_K + k:SC_K + k + 1, :] * u_scr[off:off + T, :]
    r = SC_K + CF_K
    u = acc + w_ref[r:r + 1, :]
    mu = jnp.mean(u, axis=-1, keepdims=True)
    var = jnp.mean(jnp.square(u - mu), axis=-1, keepdims=True)
    y = (u - mu) * lax.rsqrt(var + EPS) * w_ref[r + 1:r + 2, :] + w_ref[r + 2:r + 3, :]
    o_ref[:, W:] = (y * (1.0 / (1.0 + jnp.exp(-y)))).astype(o_ref.dtype)


def _conv_mixers(p, conv_w, *, n_rows, n_lat_rows, seq, ctx_len):
    T, H = CONV_TILE, CONV_HALO
    r = T // H
    n_halo_blocks = p.shape[0] // H
    cblk = CONV_COL0 // CONV_COLS
    kernel = functools.partial(_conv_kernel, tiles_lat=seq // T, tiles_ctx=ctx_len // T, n_lat_tiles=n_lat_rows // T)
    return pl.pallas_call(
        kernel,
        grid=(n_rows // T,),
        in_specs=[pl.BlockSpec((T, CONV_COLS), lambda t: (t, cblk)),
                  pl.BlockSpec((H, CONV_COLS), lambda t: (jnp.maximum(t * r - 1, 0), cblk)),
                  pl.BlockSpec((H, CONV_COLS), lambda t: (jnp.minimum((t + 1) * r, n_halo_blocks - 1), cblk)),
                  pl.BlockSpec(conv_w.shape, lambda t: (0, 0))],
        out_specs=pl.BlockSpec((T, 1024), lambda t: (t, 0)),
        out_shape=jax.ShapeDtypeStruct((n_rows, 1024), BF16),
        scratch_shapes=[pltpu.VMEM((T + 2 * H, 512), F32), pltpu.VMEM((T + 2 * H, 512), F32)],
        compiler_params=_cparams(("parallel",), 4 * T * CONV_COLS * 2 + 16 * T * 512 * 4),
        name="conv_mixers",
    )(p, p, p, conv_w)


def _merge_kernel(ya_ref, yb_ref, yc_ref, g_ref, x_ref, mod_ref, wbr_ref, wo_ref, o_ref):
    D = x_ref.shape[1]
    W = 512
    ys = (ya_ref[...], yb_ref[...], yc_ref[:, :W], yc_ref[:, W:])
    merged = jnp.zeros(x_ref.shape, F32)
    for i in range(N_BRANCH):
        z = jnp.dot(ys[i], wbr_ref[i], preferred_element_type=F32)
        gate = g_ref[:, i * D:(i + 1) * D].astype(F32)
        merged = merged + z * (1.0 / (1.0 + jnp.exp(-gate)))
    out = jnp.dot(merged.astype(BF16), wo_ref[...], preferred_element_type=F32)
    o_ref[...] = x_ref[...] + mod_ref[0, 2:3, :] * out


def _merge(ya, yb, yc, gates, x, mod_l, wbr, wo, *, n_rows, n_lat_rows, seq):
    D = x.shape[1]
    nb = mod_l.shape[0] - 1
    tm = CONV_TILE
    grp = functools.partial(_group_of, tile_rows=tm, n_lat_rows=n_lat_rows, seq=seq, n_batch=nb)
    vmem = (2 * tm * N_BRANCH * D * 2 + 4 * tm * D * 4 + N_BRANCH * 512 * D * 2 + D * D * 2
            + 4 * tm * D * 4 + 2 * tm * 2048 * 2)
    return pl.pallas_call(
        _merge_kernel,
        grid=(n_rows // tm,),
        in_specs=[pl.BlockSpec((tm, 512), lambda i: (i, 0)),
                  pl.BlockSpec((tm, 512), lambda i: (i, 0)),
                  pl.BlockSpec((tm, 1024), lambda i: (i, 0)),
                  pl.BlockSpec((tm, N_BRANCH * D), lambda i: (i, 0)),
                  pl.BlockSpec((tm, D), lambda i: (i, 0)),
                  pl.BlockSpec((1, N_MOD, D), lambda i: (grp(i), 0, 0)),
                  pl.BlockSpec(wbr.shape, lambda i: (0, 0, 0), pipeline_mode=pl.Buffered(1)),
                  pl.BlockSpec(wo.shape, lambda i: (0, 0), pipeline_mode=pl.Buffered(1))],
        out_specs=pl.BlockSpec((tm, D), lambda i: (i, 0)),
        out_shape=jax.ShapeDtypeStruct((n_rows, D), F32),
        compiler_params=_cparams(("parallel",), vmem),
        name="merge_branches",
    )(ya, yb, yc, gates, x, mod_l, wbr, wo)


def _oddeven_merge_sort_pairs(n):
    pairs = []

    def merge(lo, hi, r):
        step = r * 2
        if step < hi - lo:
            merge(lo, hi, step)
            merge(lo + r, hi, step)
            pairs.extend((i, i + r) for i in range(lo + r, hi - r, step))
        else:
            pairs.append((lo, lo + r))

    def sort(lo, hi):
        if hi - lo >= 1:
            mid = lo + (hi - lo) // 2
            sort(lo, mid)
            sort(mid + 1, hi)
            merge(lo, hi, 1)

    sort(0, n - 1)
    return pairs


def _top_values(work):
    K, SUB = PK_TOPK, 8
    T = work.shape[1]
    assert work.shape[0] % SUB == 0 and work.shape[0] <= K * SUB
    slabs = [work[SUB * i:SUB * (i + 1)] for i in range(work.shape[0] // SUB)]
    slabs += [jnp.full((SUB, T), -jnp.inf, F32)] * (K - len(slabs))

    def exchange(i, j):
        slabs[i], slabs[j] = jnp.maximum(slabs[i], slabs[j]), jnp.minimum(slabs[i], slabs[j])

    for i, j in _oddeven_merge_sort_pairs(K):
        exchange(i, j)
    for shift in (4, 2, 1):
        partner = [pltpu.roll(s, shift, 0) for s in slabs]
        slabs = [jnp.maximum(slabs[i], partner[K - 1 - i]) for i in range(K)]
        for dist in (8, 4, 2, 1):
            for i in range(K):
                if i & dist == 0:
                    exchange(i, i + dist)
    return [s[0:1] for s in slabs]


def _stack_rows(rows):
    n, T = len(rows), rows[0].shape[1]
    idx = lax.broadcasted_iota(jnp.int32, (n, T), 0)
    out = jnp.zeros((n, T), F32)
    for r, row in enumerate(rows):
        out = jnp.where(idx == r, row, out)
    return out


def _peer_select_kernel(q_ref, k1_ref, k2_ref, s1_ref, s2_ref, b_ref, c_ref, tau_ref):
    K = PK_TOPK
    nt = (((1,), (1,)), ((), ()))
    s1 = lax.dot_general(k1_ref[...], q_ref[:, :N_KEYS], nt, preferred_element_type=F32)
    s2 = lax.dot_general(k2_ref[...], q_ref[:, N_KEYS:], nt, preferred_element_type=F32)
    t1 = _top_values(s1)
    t2 = _top_values(s2)
    T1, T2 = _stack_rows(t1), _stack_rows(t2)
    ninf = jnp.full((8, s1.shape[1]), -jnp.inf, F32)
    r8 = lax.broadcasted_iota(jnp.int32, (8, s1.shape[1]), 0)
    slabs = [t1[0] + T2[0:8], t1[0] + T2[8:16], t1[1] + T2[0:8]]
    slabs.append(jnp.where(r8 < K // 3, t1[2] + T2[0:8], ninf))
    slabs.append(jnp.where(r8 < K // 4, t1[3] + T2[0:8], ninf))
    slabs.append(T1[8:16] + t2[0])
    slabs.append(jnp.where(r8 >= 4, T1[0:8] + t2[0], ninf))
    slabs.append(jnp.where(r8 >= 4, T1[0:8] + t2[1], ninf))
    slabs.append(jnp.where(r8 == 4, T1[0:8] + t2[2], ninf))
    cand = jnp.concatenate(slabs, axis=0)
    best = _top_values(cand)
    top, tau = best[0], best[K - 1]
    z = jnp.sum(jnp.where(cand >= tau, jnp.exp(cand - top), 0.0), axis=0, keepdims=True)
    s1_ref[0] = s1
    s2_ref[0] = s2
    b_ref[0] = jnp.exp(s1 - t1[0]) / z
    c_ref[0] = jnp.exp(s2 - t2[0])
    tau_ref[0] = tau


def _peer_select(q, k1, k2):
    n = q.shape[0]
    tm = ROW_TILE
    big = jax.ShapeDtypeStruct((PK_HEADS, N_KEYS, n), F32)
    big_spec = pl.BlockSpec((1, N_KEYS, tm), lambda i, h: (h, 0, i))
    return pl.pallas_call(
        _peer_select_kernel,
        grid=(n // tm, PK_HEADS),
        in_specs=[pl.BlockSpec((tm, 2 * N_KEYS), lambda i, h: (i, h)),
                  pl.BlockSpec(k1.shape, lambda i, h: (0, 0)),
                  pl.BlockSpec(k2.shape, lambda i, h: (0, 0))],
        out_specs=[big_spec, big_spec, big_spec, big_spec, pl.BlockSpec((1, 1, tm), lambda i, h: (h, 0, i))],
        out_shape=[big, big, big, big, jax.ShapeDtypeStruct((PK_HEADS, 1, n), F32)],
        compiler_params=_cparams(("parallel", "parallel"), 8 * N_KEYS * tm * 4 + 32 * N_KEYS * tm * 4),
        name="peer_select",
    )(q, k1, k2)


def _gelu_tanh(x):
    return 0.5 * x * (1.0 + jnp.tanh(math.sqrt(2.0 / math.pi) * (x + 0.044715 * (x * x * x))))


def _peer_dense_kernel(h_ref, u_ref, vt_ref, s1r_ref, br_ref, s2_ref, c_ref, tau_ref, x_ref, mod_ref, o_ref,
                       acc_scr, *slot_scr, nj):
    t = pl.program_id(0)
    tm = h_ref.shape[0]
    jc = jnp.maximum(t - 2, 0) % nj

    @pl.when(t == 0)
    def _():
        for ref in slot_scr:
            ref[...] = jnp.zeros(ref.shape, ref.dtype)

    @pl.when(jc == 0)
    def _():
        acc_scr[...] = jnp.zeros(acc_scr.shape, F32)

    def tick(s_write, s_read, w_write, w_read):
        n_slabs = u_ref.shape[0] // N_KEYS
        n_key_blocks = 8
        kb = N_KEYS // n_key_blocks

        def stage_b(key_block, lc):
            keys = slice(key_block * kb, (key_block + 1) * kb)
            cols = slice(lc * LANES, (lc + 1) * LANES)
            gates = [None] * n_slabs
            for hd in range(PK_HEADS):
                s2, cf, tau = s2_ref[hd, keys, cols], c_ref[hd, keys, cols], tau_ref[hd, :, cols]
                for b in range(n_slabs):
                    total = s2 + s1r_ref[b, hd:hd + 1, cols]
                    term = jnp.where(total >= tau, cf * br_ref[b, hd:hd + 1, cols], 0.0)
                    gates[b] = term if gates[b] is None else gates[b] + term
            for b in range(n_slabs):
                rows = slice(b * N_KEYS + keys.start, b * N_KEYS + keys.stop)
                w_write[rows, cols] = (gates[b] * _gelu_tanh(s_read[rows, cols])).astype(BF16)

        nt = (((1,), (1,)), ((), ()))
        kp_size = 256
        n_a_pieces = u_ref.shape[1] // kp_size
        n_c_pieces = vt_ref.shape[2] // kp_size
        n_c_blocks = 8
        c_rows = acc_scr.shape[0] // n_c_blocks
        units = [(key_block, lc) for key_block in range(n_key_blocks) for lc in range(tm // LANES)]
        n_groups = n_c_blocks * n_c_pieces
        a_per_group = 2 * n_a_pieces // n_groups
        assert a_per_group * n_groups == 2 * n_a_pieces
        a_acc, c_acc = None, None
        for g in range(n_groups):
            for i in range(a_per_group):
                half, piece = divmod(g * a_per_group + i, n_a_pieces)
                cols = slice(half * (tm // 2), (half + 1) * (tm // 2))
                ks = slice(piece * kp_size, (piece + 1) * kp_size)
                d = lax.dot_general(u_ref[:, ks], h_ref[cols, ks], nt, preferred_element_type=F32)
                a_acc = d if piece == 0 else a_acc + d
                if piece == n_a_pieces - 1:
                    s_write[:, cols] = a_acc
            blk, piece = divmod(g, n_c_pieces)
            rows = slice(blk * c_rows, (blk + 1) * c_rows)
            ks = slice(piece * kp_size, (piece + 1) * kp_size)
            if piece == 0:
                c_acc = acc_scr[rows, :]
            c_acc = c_acc + jnp.dot(vt_ref[0, rows, ks], w_read[ks, :], preferred_element_type=F32)
            if piece == n_c_pieces - 1:
                acc_scr[rows, :] = c_acc
            for _ in range(len(units) // (n_groups - g)):
                stage_b(*units.pop(0))

    parity = t % 2
    if len(slot_scr) == 2:
        s_scr, w_scr = slot_scr
        tick(s_scr.at[parity], s_scr.at[1 - parity], w_scr.at[1 - parity], w_scr.at[parity])
    else:
        s0_scr, s1_scr, w0_scr, w1_scr = slot_scr
        pl.when(parity == 0)(functools.partial(tick, s0_scr, s1_scr, w1_scr, w0_scr))
        pl.when(parity == 1)(functools.partial(tick, s1_scr, s0_scr, w0_scr, w1_scr))

    @pl.when(jnp.logical_and(jc == nj - 1, t >= 2))
    def _():
        o_ref[...] = x_ref[...] + mod_ref[0, 5:6, :] * acc_scr[...].T


def _peer_dense(h, u, vt, s1r, br, s2, c, tau, x, mod_l, *, n_rows, n_lat_rows, seq, single_body):
    D = x.shape[1]
    n_exp = u.shape[0]
    nb = mod_l.shape[0] - 1
    tm, te = ROW_TILE, PEER_TE
    nk = te // N_KEYS
    nj = n_exp // te
    n_items = (n_rows // tm) * nj
    grp = functools.partial(_group_of, tile_rows=tm, n_lat_rows=n_lat_rows, seq=seq, n_batch=nb)

    def item(t, lag):
        k = jnp.clip(t - lag, 0, n_items - 1)
        return k // nj, k % nj

    sel_spec = pl.BlockSpec((PK_HEADS, N_KEYS, tm), lambda t: (0, 0, item(t, 1)[0]))
    row_spec = pl.BlockSpec((nk, PK_HEADS, tm), lambda t: (item(t, 1)[1], 0, item(t, 1)[0]))
    vmem = (2 * tm * D * 2 + 4 * te * D * 2 + 4 * PK_HEADS * N_KEYS * tm * 4 + 4 * tm * D * 4
            + D * tm * 4 + 2 * te * tm * 6 + 2 * D * tm * 4)
    return pl.pallas_call(
        functools.partial(_peer_dense_kernel, nj=nj),
        grid=(n_items + 2,),
        in_specs=[pl.BlockSpec((tm, D), lambda t: (item(t, 0)[0], 0)),
                  pl.BlockSpec((te, D), lambda t: (item(t, 0)[1], 0)),
                  pl.BlockSpec((1, D, te), lambda t: (item(t, 2)[1], 0, 0)),
                  row_spec, row_spec, sel_spec, sel_spec,
                  pl.BlockSpec((PK_HEADS, 1, tm), lambda t: (0, 0, item(t, 1)[0])),
                  pl.BlockSpec((tm, D), lambda t: (item(t, 2)[0], 0)),
                  pl.BlockSpec((1, N_MOD, D), lambda t: (grp(item(t, 2)[0]), 0, 0))],
        out_specs=pl.BlockSpec((tm, D), lambda t: (item(t, 2)[0], 0)),
        out_shape=jax.ShapeDtypeStruct((n_rows, D), F32),
        scratch_shapes=[pltpu.VMEM((D, tm), F32)] + (
            [pltpu.VMEM((2, te, tm), F32), pltpu.VMEM((2, te, tm), BF16)] if single_body else
            [pltpu.VMEM((te, tm), F32), pltpu.VMEM((te, tm), F32), pltpu.VMEM((te, tm), BF16),
             pltpu.VMEM((te, tm), BF16)]),
        compiler_params=_cparams(("arbitrary",), vmem),
        name="peer_dense",
    )(h, u, vt, s1r, br, s2, c, tau, x, mod_l)


def _rope_tables(seq, pad_rows):
    t = jnp.arange(seq, dtype=jnp.int32)
    row = (t // GRID_W).astype(F32)
    col = (t % GRID_W).astype(F32)

    def tables(dim, lanes_per_pair):
        n_pair = dim // 4
        inv = ROPE_THETA ** (-jnp.arange(n_pair, dtype=F32) / n_pair)
        ang = jnp.concatenate([row[:, None] * inv, col[:, None] * inv], axis=-1)
        lane_pair = np.arange(LANES) % lanes_per_pair
        sign = np.where(np.arange(LANES) < 64, -1.0, 1.0).astype(np.float32)
        return jnp.cos(ang)[:, lane_pair], jnp.sin(ang)[:, lane_pair] * sign

    ca, sa = tables(64, 32)
    cb, sb = tables(128, 64)
    tab = jnp.stack([ca, sa, cb, sb])
    ident = jnp.stack([jnp.ones((pad_rows, LANES), F32), jnp.zeros((pad_rows, LANES), F32)] * 2)
    return jnp.concatenate([tab, ident], axis=1)


def _permuted_proj_weight(w):
    D = w.shape[0]

    def da(blk):
        return blk.reshape(D, DA_HEADS, 2, 32, 2).transpose(0, 1, 4, 2, 3).reshape(D, DA_HEADS * LANES)

    def gq(blk):
        return blk.reshape(D, -1, 64, 2).transpose(0, 1, 3, 2).reshape(D, blk.shape[1])

    return jnp.concatenate([da(w[:, 0:512]), da(w[:, 512:1024]), gq(w[:, 1536:2048]), gq(w[:, 2048:2304]),
                            w[:, 1024:1536], w[:, 2304:5120]], axis=1)


def _qk_gains(da_qn_g, da_kn_g, gq_qn_g, gq_kn_g):
    da, gq = _da_lane_perm() % 64, _gq_lane_perm()
    log2e = math.log2(math.e)
    rows = ([da_qn_g[da] * (64 ** -0.5 * log2e)] * DA_HEADS + [da_kn_g[da]] * DA_HEADS
            + [gq_qn_g[gq] * (128 ** -0.5 * log2e)] * GQ_HEADS + [gq_kn_g[gq]] * GQ_KV)
    return jnp.stack(rows).astype(F32)


def kernel(x, c, ctx, c_ctx, mod_w, mod_b, norm1_g, norm2_g, w_in, da_qn_g, da_kn_g, da_lam, da_head_g,
           gq_qn_g, gq_kn_g, sc_w, cf_w, cf_b, cf_ln_g, cf_ln_b, w_branch, w_out,
           pk_wq, pk_k1, pk_k2, pk_u, pk_v):
    B, S, D = x.shape
    C = ctx.shape[1]
    L = mod_w.shape[0]
    n_lat, n_all = B * S, B * S + B * C
    assert S % ROW_TILE == 0 and (B * C) % ROW_TILE == 0 and C % CONV_TILE == 0 and S % min(ATT_TK, S) == 0
    assert B + 1 <= 8 and D == N_BRANCH * 512 and w_in.shape[2] == 5120 + N_BRANCH * D

    xs = jnp.concatenate([x.reshape(n_lat, D), ctx.reshape(B * C, D)], axis=0)
    cs = jnp.zeros((8, D), F32).at[:B].set(c).at[B].set(c_ctx)
    mod = _modulation(cs, mod_w, mod_b).reshape(L, 8, N_MOD, D)[:, :B + 1]
    rope_tab = _rope_tables(S, ROW_TILE)
    geom = dict(n_lat_rows=n_lat, seq=S)

    for l in range(L):
        with_ctx = l < L - 1
        n_rows = n_all if with_ctx else n_lat
        lam_init = 0.8 - 0.6 * math.exp(-0.3 * l)
        lp = da_lam[l].astype(F32)
        lam = jnp.exp(jnp.sum(lp[0] * lp[1])) - jnp.exp(jnp.sum(lp[2] * lp[3])) + lam_init
        aux = jnp.zeros((8, LANES), F32).at[0].set(lam).at[1].set(da_head_g[l] * (1.0 - lam_init))
        w_p = _permuted_proj_weight(w_in[l][:, :5120]).astype(BF16)
        w_g = w_in[l][:, 5120:].astype(BF16)
        conv_w = jnp.concatenate([sc_w[l], cf_w[l], cf_b[l][None], cf_ln_g[l][None], cf_ln_b[l][None],
                                  jnp.zeros((3, 512), F32)], axis=0)
        mod_l = mod[l]

        p = _norm_mod_matmul(xs, norm1_g[l], mod_l, w_p, shift=0, scale=1, n_rows=n_all, emit_h=False,
                             name="in_proj", **geom)
        gates = _norm_mod_matmul(xs, norm1_g[l], mod_l, w_g, shift=0, scale=1, n_rows=n_rows, emit_h=False,
                                 name="gate_proj", **geom)
        qk = _qk_prep(p, rope_tab, _qk_gains(da_qn_g[l], da_kn_g[l], gq_qn_g[l], gq_kn_g[l]), **geom)
        att = dict(n_batch=B, seq=S, ctx_len=C, with_ctx_queries=with_ctx)
        ya = _attention(qk, p, aux, diff=True, **att)
        yb = _attention(qk, p, aux, diff=False, **att)
        yc = _conv_mixers(p, conv_w, n_rows=n_rows, ctx_len=C, **geom)
        xs = _merge(ya, yb, yc, gates, xs, mod_l, w_branch[l].astype(BF16), w_out[l].astype(BF16),
                    n_rows=n_rows, **geom)

        q, h2 = _norm_mod_matmul(xs, norm2_g[l], mod_l, pk_wq[l].astype(BF16), shift=3, scale=4, n_rows=n_rows,
                                 emit_h=True, name="peer_query", **geom)
        s1, s2, bf, cf, tau = _peer_select(q, pk_k1[l].astype(BF16), pk_k2[l].astype(BF16))
        s1r, br = jnp.transpose(s1, (1, 0, 2)), jnp.transpose(bf, (1, 0, 2))
        vt = pk_v[l].reshape(-1, PEER_TE, D).transpose(0, 2, 1).astype(BF16)
        xs = _peer_dense(h2, pk_u[l].astype(BF16), vt, s1r, br, s2, cf, tau, xs, mod_l, single_body=l >= L // 2,
                         n_rows=n_rows, **geom)
    return xs[:n_lat].reshape(B, S, D)
```

```python
import functools
import math

import numpy as np
import jax
import jax.numpy as jnp
from jax import lax
from jax.experimental import pallas as pl
from jax.experimental.pallas import tpu as pltpu

F32 = jnp.float32
BF16 = jnp.bfloat16

EPS = 1e-6
N_MOD = 6
GRID_W = 64
ROPE_THETA = 10000.0
N_BRANCH = 4
DA_HEADS = 4
GQ_HEADS = 4
GQ_KV = 2
SC_K = 3
CF_K = 31
PK_HEADS = 8
N_KEYS = 128
PK_TOPK = 16

LANES = 128
V7X_VMEM_BYTES = 64 * 1024 * 1024
VMEM_CAP = V7X_VMEM_BYTES - 8 * 1024 * 1024

ROW_TILE = 512
CONV_TILE = 256
CONV_HALO = 16
ATT_TQ = 256
ATT_TK = 1024
PEER_TE = 512

QK_COLS = (2 * DA_HEADS + GQ_HEADS + GQ_KV) * LANES
V_DA_BLK = QK_COLS // LANES
V_GQ_BLK = V_DA_BLK + DA_HEADS
CONV_COL0 = (V_GQ_BLK + GQ_KV) * LANES
CONV_COLS = 5 * 512
P_COLS = CONV_COL0 + CONV_COLS


def _cparams(sem, vmem_bytes):
    limit = int(min(max(vmem_bytes * 5 // 4 + (4 << 20), 16 << 20), VMEM_CAP))
    return pltpu.CompilerParams(dimension_semantics=sem, vmem_limit_bytes=limit)


def _group_of(tile, tile_rows, n_lat_rows, seq, n_batch):
    n_lat_tiles = n_lat_rows // tile_rows
    return jnp.where(tile >= n_lat_tiles, n_batch, tile // (seq // tile_rows))


def _mod_kernel(c_ref, w_ref, b_ref, o_ref):
    c = c_ref[...]
    sc = c * (1.0 / (1.0 + jnp.exp(-c)))
    o_ref[0] = jnp.dot(sc.astype(BF16), w_ref[0].astype(BF16), preferred_element_type=F32) + b_ref[0]


def _modulation(cs, mod_w, mod_b):
    L, D, W = mod_w.shape
    tn = 1024
    return pl.pallas_call(
        _mod_kernel,
        grid=(L, W // tn),
        in_specs=[pl.BlockSpec((8, D), lambda l, j: (0, 0)),
                  pl.BlockSpec((1, D, tn), lambda l, j: (l, 0, j)),
                  pl.BlockSpec((1, 1, tn), lambda l, j: (l, 0, j))],
        out_specs=pl.BlockSpec((1, 8, tn), lambda l, j: (l, 0, j)),
        out_shape=jax.ShapeDtypeStruct((L, 8, W), F32),
        compiler_params=_cparams(("parallel", "parallel"), 2 * D * tn * 4 + D * tn * 2),
        name="modulation",
    )(cs, mod_w, mod_b.reshape(L, 1, W))


def _nmm_kernel(x_ref, g_ref, mod_ref, w_ref, *rest, shift, scale, emit_h):
    if emit_h:
        o_ref, h_ref, h_scr = rest
    else:
        o_ref, h_scr = rest

    @pl.when(pl.program_id(1) == 0)
    def _():
        x = x_ref[...]
        ms = jnp.mean(x * x, axis=-1, keepdims=True)
        y = x * lax.rsqrt(ms + EPS) * g_ref[...]
        h = y * (1.0 + mod_ref[0, scale:scale + 1, :]) + mod_ref[0, shift:shift + 1, :]
        h_scr[...] = h.astype(BF16)
        if emit_h:
            h_ref[...] = h_scr[...]

    o_ref[...] = jnp.dot(h_scr[...], w_ref[...], preferred_element_type=F32).astype(o_ref.dtype)


def _norm_mod_matmul(x, g, mod_l, w, *, shift, scale, n_rows, n_lat_rows, seq, emit_h, name):
    D = x.shape[1]
    cols = w.shape[1]
    nb = mod_l.shape[0] - 1
    tm, tn = ROW_TILE, (1024 if cols % 1024 == 0 else 512)
    grp = functools.partial(_group_of, tile_rows=tm, n_lat_rows=n_lat_rows, seq=seq, n_batch=nb)
    out_shape = [jax.ShapeDtypeStruct((n_rows, cols), BF16)]
    out_specs = [pl.BlockSpec((tm, tn), lambda i, j: (i, j))]
    if emit_h:
        out_shape.append(jax.ShapeDtypeStruct((n_rows, D), BF16))
        out_specs.append(pl.BlockSpec((tm, D), lambda i, j: (i, 0)))
    vmem = 2 * tm * D * 4 + 2 * D * tn * 2 + 2 * tm * tn * 2 + 3 * tm * D * 2 + 2 * tm * D * 4
    res = pl.pallas_call(
        functools.partial(_nmm_kernel, shift=shift, scale=scale, emit_h=emit_h),
        grid=(n_rows // tm, cols // tn),
        in_specs=[pl.BlockSpec((tm, D), lambda i, j: (i, 0)),
                  pl.BlockSpec((1, D), lambda i, j: (0, 0)),
                  pl.BlockSpec((1, N_MOD, D), lambda i, j: (grp(i), 0, 0)),
                  pl.BlockSpec((D, tn), lambda i, j: (0, j))],
        out_specs=out_specs,
        out_shape=out_shape,
        scratch_shapes=[pltpu.VMEM((tm, D), BF16)],
        compiler_params=_cparams(("parallel", "arbitrary"), vmem),
        name=name,
    )(x, g.reshape(1, D), mod_l, w)
    return res if emit_h else res[0]


def _da_lane_perm():
    j = np.arange(LANES)
    comp, parity, pair = (j // 32) % 2, j // 64, j % 32
    return comp * 64 + 2 * pair + parity


def _gq_lane_perm():
    j = np.arange(LANES)
    return 2 * (j % 64) + j // 64


def _qk_prep_kernel(p_ref, tab_ref, gain_ref, o_ref):
    lane = lax.broadcasted_iota(jnp.int32, (1, LANES), 1)
    comp0 = (lane % 64) < 32
    n_da = 2 * DA_HEADS
    for hd in range(QK_COLS // LANES):
        sl = slice(hd * LANES, (hd + 1) * LANES)
        x = p_ref[:, sl].astype(F32)
        xx = x * x
        tot = jnp.sum(xx, axis=-1, keepdims=True)
        if hd < n_da:
            s0 = jnp.sum(jnp.where(comp0, xx, 0.0), axis=-1, keepdims=True)
            ms = jnp.where(comp0, s0, tot - s0) * (1.0 / 64.0)
            cos, sin = tab_ref[0], tab_ref[1]
        else:
            ms = tot * (1.0 / 128.0)
            cos, sin = tab_ref[2], tab_ref[3]
        y = x * lax.rsqrt(ms + EPS) * gain_ref[hd:hd + 1, :]
        o_ref[:, sl] = (y * cos + pltpu.roll(y, 64, 1) * sin).astype(BF16)


def _qk_prep(p, rope_tab, gains, *, n_lat_rows, seq):
    n = p.shape[0]
    tm = ROW_TILE
    n_lat_tiles = n_lat_rows // tm
    rope_blk = lambda i: jnp.where(i >= n_lat_tiles, seq // tm, i % (seq // tm))
    return pl.pallas_call(
        _qk_prep_kernel,
        grid=(n // tm,),
        in_specs=[pl.BlockSpec((tm, QK_COLS), lambda i: (i, 0)),
                  pl.BlockSpec((4, tm, LANES), lambda i: (0, rope_blk(i), 0)),
                  pl.BlockSpec((QK_COLS // LANES, LANES), lambda i: (0, 0))],
        out_specs=pl.BlockSpec((tm, QK_COLS), lambda i: (i, 0)),
        out_shape=jax.ShapeDtypeStruct((n, QK_COLS), BF16),
        compiler_params=_cparams(("parallel",), 4 * tm * QK_COLS * 2 + 8 * tm * LANES * 4 + 6 * tm * LANES * 4),
        name="qk_prep",
    )(p, rope_tab, gains)


def _flash_kernel(q_ref, kl_ref, vl_ref, kc_ref, vc_ref, aux_ref, o_ref, *, diff, n_lat_chunks, nq_lat, tk,
                  ctx_queries):
    tq = q_ref.shape[0]
    lane = lax.broadcasted_iota(jnp.int32, (1, LANES), 1)
    if diff:
        q = q_ref[...]
        comp0 = (lane % 64) < 32
        zero = jnp.zeros_like(q)
        q2 = jnp.concatenate([jnp.where(comp0, q, zero), jnp.where(comp0, zero, q)], axis=0)
    else:
        q2 = jnp.concatenate([q_ref[:, :LANES], q_ref[:, LANES:]], axis=0)

    def scores(k):
        return lax.dot_general(q2, k, (((1,), (1,)), ((), ())), preferred_element_type=F32)

    def weighted(p, v):
        ones_col = jnp.broadcast_to(jnp.where(lane == 0, 1.0, 0.0).astype(BF16), v.shape)
        return jnp.dot(p.astype(BF16), jnp.concatenate([v, ones_col], axis=1), preferred_element_type=F32)

    def fold(carry, parts):
        m_new = functools.reduce(jnp.maximum, [jnp.max(s, axis=-1, keepdims=True) for s, _ in parts])
        if carry is not None:
            m_new = jnp.maximum(carry[0], m_new)
        acc = None
        for s, v in parts:
            pv = weighted(jnp.exp2(s - m_new), v)
            acc = pv if acc is None else acc + pv
        if carry is not None:
            acc = acc + jnp.exp2(carry[0] - m_new) * carry[1]
        return m_new, acc

    def finish(carry):
        acc = carry[1]
        o = acc[:, :LANES] / acc[:, LANES:LANES + 1]
        if diff:
            d = o[:tq] - aux_ref[0:1, :] * o[tq:]
            ms = jnp.mean(d * d, axis=-1, keepdims=True)
            o_ref[...] = (d * lax.rsqrt(ms + EPS) * aux_ref[1:2, :]).astype(o_ref.dtype)
        else:
            o_ref[:, :LANES] = o[:tq].astype(o_ref.dtype)
            o_ref[:, LANES:] = o[tq:].astype(o_ref.dtype)

    def latent_queries():
        carry = None
        for c in range(n_lat_chunks):
            parts = [(scores(kl_ref[c * tk:(c + 1) * tk, :]), vl_ref[c * tk:(c + 1) * tk, :])]
            if c == n_lat_chunks - 1:
                parts.append((scores(kc_ref[...]), vc_ref[...]))
            carry = fold(carry, parts)
        finish(carry)

    if ctx_queries:
        is_lat = pl.program_id(2) < nq_lat
        pl.when(is_lat)(latent_queries)

        @pl.when(jnp.logical_not(is_lat))
        def _():
            finish(fold(None, [(scores(kc_ref[...]), vc_ref[...])]))
    else:
        latent_queries()


def _attention(qk, p, aux, *, diff, n_batch, seq, ctx_len, with_ctx_queries):
    tq, tk = ATT_TQ, min(ATT_TK, seq)
    nq_lat, nq_ctx = seq // tq, ctx_len // tq
    n_lat_rows = n_batch * seq
    if diff:
        n_heads, qw = DA_HEADS, LANES
        q_blk = lambda h: h
        k_blk = lambda h: DA_HEADS + h
        v_blk = lambda h: V_DA_BLK + h
    else:
        n_heads, qw = GQ_KV, 2 * LANES
        q_blk = lambda h: (2 * DA_HEADS * LANES) // qw + h
        k_blk = lambda h: 2 * DA_HEADS + GQ_HEADS + h
        v_blk = lambda h: V_GQ_BLK + h
    nq = nq_lat + (nq_ctx if with_ctx_queries else 0)
    q_row = lambda b, qi: jnp.where(qi >= nq_lat, n_batch * nq_lat + b * nq_ctx + qi - nq_lat, b * nq_lat + qi)
    ctx_row = lambda b: n_lat_rows // ctx_len + b
    kernel = functools.partial(_flash_kernel, diff=diff, n_lat_chunks=seq // tk, nq_lat=nq_lat, tk=tk,
                               ctx_queries=with_ctx_queries)
    n_out = n_lat_rows + (n_batch * ctx_len if with_ctx_queries else 0)
    vmem = 4 * seq * LANES * 2 + 2 * tq * (tk + ctx_len) * 4 * 4 + 8 * tq * LANES * 4 * 3
    return pl.pallas_call(
        kernel,
        grid=(n_batch, n_heads, nq),
        in_specs=[pl.BlockSpec((tq, qw), lambda b, h, qi: (q_row(b, qi), q_blk(h))),
                  pl.BlockSpec((seq, LANES), lambda b, h, qi: (b, k_blk(h))),
                  pl.BlockSpec((seq, LANES), lambda b, h, qi: (b, v_blk(h))),
                  pl.BlockSpec((ctx_len, LANES), lambda b, h, qi: (ctx_row(b), k_blk(h))),
                  pl.BlockSpec((ctx_len, LANES), lambda b, h, qi: (ctx_row(b), v_blk(h))),
                  pl.BlockSpec((8, LANES), lambda b, h, qi: (0, 0))],
        out_specs=pl.BlockSpec((tq, qw), lambda b, h, qi: (q_row(b, qi), h)),
        out_shape=jax.ShapeDtypeStruct((n_out, n_heads * qw), BF16),
        compiler_params=_cparams(("parallel", "parallel", "parallel"), vmem),
        name="diff_attention" if diff else "gqa_attention",
    )(qk, qk, p, qk, p, aux)


def _conv_kernel(cur_ref, prev_ref, next_ref, w_ref, o_ref, z_scr, u_scr, *, tiles_lat, tiles_ctx, n_lat_tiles):
    t = pl.program_id(0)
    in_lat = t < n_lat_tiles
    pos = jnp.where(in_lat, t % tiles_lat, (t - n_lat_tiles) % tiles_ctx)
    last = jnp.where(in_lat, tiles_lat, tiles_ctx) - 1
    has_prev = (pos > 0).astype(F32)
    has_next = (pos < last).astype(F32)
    T, H = cur_ref.shape[0], CONV_HALO
    W = 512

    def cols(ref, k):
        return ref[:, k * W:(k + 1) * W].astype(F32)

    def glu(ref):
        return cols(ref, 3) * (1.0 / (1.0 + jnp.exp(-cols(ref, 4))))

    z_scr[0:H, :] = cols(prev_ref, 1) * cols(prev_ref, 2) * has_prev
    z_scr[H:H + T, :] = cols(cur_ref, 1) * cols(cur_ref, 2)
    z_scr[H + T:, :] = cols(next_ref, 1) * cols(next_ref, 2) * has_next
    acc = jnp.zeros((T, W), F32)
    for k in range(SC_K):
        off = H + k - SC_K // 2
        acc = acc + w_ref[k:k + 1, :] * z_scr[off:off + T, :]
    o_ref[:, :W] = (cols(cur_ref, 0) * acc).astype(o_ref.dtype)

    u_scr[0:H, :] = glu(prev_ref) * has_prev
    u_scr[H:H + T, :] = glu(cur_ref)
    u_scr[H + T:, :] = glu(next_ref) * has_next
    acc = jnp.zeros((T, W), F32)
    for k in range(CF_K):
        off = H + k - CF_K // 2
        acc = acc + w_ref[SC_K + k:SC_K + k + 1, :] * u_scr[off:off + T, :]
    r = SC_K + CF_K
    u = acc + w_ref[r:r + 1, :]
    mu = jnp.mean(u, axis=-1, keepdims=True)
    var = jnp.mean(jnp.square(u - mu), axis=-1, keepdims=True)
    y = (u - mu) * lax.rsqrt(var + EPS) * w_ref[r + 1:r + 2, :] + w_ref[r + 2:r + 3, :]
    o_ref[:, W:] = (y * (1.0 / (1.0 + jnp.exp(-y)))).astype(o_ref.dtype)


def _conv_mixers(p, conv_w, *, n_rows, n_lat_rows, seq, ctx_len):
    T, H = CONV_TILE, CONV_HALO
    r = T // H
    n_halo_blocks = p.shape[0] // H
    cblk = CONV_COL0 // CONV_COLS
    kernel = functools.partial(_conv_kernel, tiles_lat=seq // T, tiles_ctx=ctx_len // T, n_lat_tiles=n_lat_rows // T)
    return pl.pallas_call(
        kernel,
        grid=(n_rows // T,),
        in_specs=[pl.BlockSpec((T, CONV_COLS), lambda t: (t, cblk)),
                  pl.BlockSpec((H, CONV_COLS), lambda t: (jnp.maximum(t * r - 1, 0), cblk)),
                  pl.BlockSpec((H, CONV_COLS), lambda t: (jnp.minimum((t + 1) * r, n_halo_blocks - 1), cblk)),
                  pl.BlockSpec(conv_w.shape, lambda t: (0, 0))],
        out_specs=pl.BlockSpec((T, 1024), lambda t: (t, 0)),
        out_shape=jax.ShapeDtypeStruct((n_rows, 1024), BF16),
        scratch_shapes=[pltpu.VMEM((T + 2 * H, 512), F32), pltpu.VMEM((T + 2 * H, 512), F32)],
        compiler_params=_cparams(("parallel",), 4 * T * CONV_COLS * 2 + 16 * T * 512 * 4),
        name="conv_mixers",
    )(p, p, p, conv_w)


def _merge_kernel(ya_ref, yb_ref, yc_ref, g_ref, x_ref, mod_ref, wbr_ref, wo_ref, o_ref):
    D = x_ref.shape[1]
    W = 512
    ys = (ya_ref[...], yb_ref[...], yc_ref[:, :W], yc_ref[:, W:])
    merged = jnp.zeros(x_ref.shape, F32)
    for i in range(N_BRANCH):
        z = jnp.dot(ys[i], wbr_ref[i], preferred_element_type=F32)
        gate = g_ref[:, i * D:(i + 1) * D].astype(F32)
        merged = merged + z * (1.0 / (1.0 + jnp.exp(-gate)))
    out = jnp.dot(merged.astype(BF16), wo_ref[...], preferred_element_type=F32)
    o_ref[...] = x_ref[...] + mod_ref[0, 2:3, :] * out


def _merge(ya, yb, yc, gates, x, mod_l, wbr, wo, *, n_rows, n_lat_rows, seq):
    D = x.shape[1]
    nb = mod_l.shape[0] - 1
    tm = CONV_TILE
    grp = functools.partial(_group_of, tile_rows=tm, n_lat_rows=n_lat_rows, seq=seq, n_batch=nb)
    vmem = (2 * tm * N_BRANCH * D * 2 + 4 * tm * D * 4 + N_BRANCH * 512 * D * 2 + D * D * 2
            + 4 * tm * D * 4 + 2 * tm * 2048 * 2)
    return pl.pallas_call(
        _merge_kernel,
        grid=(n_rows // tm,),
        in_specs=[pl.BlockSpec((tm, 512), lambda i: (i, 0)),
                  pl.BlockSpec((tm, 512), lambda i: (i, 0)),
                  pl.BlockSpec((tm, 1024), lambda i: (i, 0)),
                  pl.BlockSpec((tm, N_BRANCH * D), lambda i: (i, 0)),
                  pl.BlockSpec((tm, D), lambda i: (i, 0)),
                  pl.BlockSpec((1, N_MOD, D), lambda i: (grp(i), 0, 0)),
                  pl.BlockSpec(wbr.shape, lambda i: (0, 0, 0), pipeline_mode=pl.Buffered(1)),
                  pl.BlockSpec(wo.shape, lambda i: (0, 0), pipeline_mode=pl.Buffered(1))],
        out_specs=pl.BlockSpec((tm, D), lambda i: (i, 0)),
        out_shape=jax.ShapeDtypeStruct((n_rows, D), F32),
        compiler_params=_cparams(("parallel",), vmem),
        name="merge_branches",
    )(ya, yb, yc, gates, x, mod_l, wbr, wo)


def _oddeven_merge_sort_pairs(n):
    pairs = []

    def merge(lo, hi, r):
        step = r * 2
        if step < hi - lo:
            merge(lo, hi, step)
            merge(lo + r, hi, step)
            pairs.extend((i, i + r) for i in range(lo + r, hi - r, step))
        else:
            pairs.append((lo, lo + r))

    def sort(lo, hi):
        if hi - lo >= 1:
            mid = lo + (hi - lo) // 2
            sort(lo, mid)
            sort(mid + 1, hi)
            merge(lo, hi, 1)

    sort(0, n - 1)
    return pairs


def _top_values(work):
    K, SUB = PK_TOPK, 8
    T = work.shape[1]
    assert work.shape[0] % SUB == 0 and work.shape[0] <= K * SUB
    slabs = [work[SUB * i:SUB * (i + 1)] for i in range(work.shape[0] // SUB)]
    slabs += [jnp.full((SUB, T), -jnp.inf, F32)] * (K - len(slabs))

    def exchange(i, j):
        slabs[i], slabs[j] = jnp.maximum(slabs[i], slabs[j]), jnp.minimum(slabs[i], slabs[j])

    for i, j in _oddeven_merge_sort_pairs(K):
        exchange(i, j)
    for shift in (4, 2, 1):
        partner = [pltpu.roll(s, shift, 0) for s in slabs]
        slabs = [jnp.maximum(slabs[i], partner[K - 1 - i]) for i in range(K)]
        for dist in (8, 4, 2, 1):
            for i in range(K):
                if i & dist == 0:
                    exchange(i, i + dist)
    return [s[0:1] for s in slabs]


def _stack_rows(rows):
    n, T = len(rows), rows[0].shape[1]
    idx = lax.broadcasted_iota(jnp.int32, (n, T), 0)
    out = jnp.zeros((n, T), F32)
    for r, row in enumerate(rows):
        out = jnp.where(idx == r, row, out)
    return out


def _peer_select_kernel(q_ref, k1_ref, k2_ref, s1_ref, s2_ref, b_ref, c_ref, tau_ref):
    K = PK_TOPK
    nt = (((1,), (1,)), ((), ()))
    s1 = lax.dot_general(k1_ref[...], q_ref[:, :N_KEYS], nt, preferred_element_type=F32)
    s2 = lax.dot_general(k2_ref[...], q_ref[:, N_KEYS:], nt, preferred_element_type=F32)
    t1 = _top_values(s1)
    t2 = _top_values(s2)
    T1, T2 = _stack_rows(t1), _stack_rows(t2)
    ninf = jnp.full((8, s1.shape[1]), -jnp.inf, F32)
    r8 = lax.broadcasted_iota(jnp.int32, (8, s1.shape[1]), 0)
    slabs = [t1[0] + T2[0:8], t1[0] + T2[8:16], t1[1] + T2[0:8]]
    slabs.append(jnp.where(r8 < K // 3, t1[2] + T2[0:8], ninf))
    slabs.append(jnp.where(r8 < K // 4, t1[3] + T2[0:8], ninf))
    slabs.append(T1[8:16] + t2[0])
    slabs.append(jnp.where(r8 >= 4, T1[0:8] + t2[0], ninf))
    slabs.append(jnp.where(r8 >= 4, T1[0:8] + t2[1], ninf))
    slabs.append(jnp.where(r8 == 4, T1[0:8] + t2[2], ninf))
    cand = jnp.concatenate(slabs, axis=0)
    best = _top_values(cand)
    top, tau = best[0], best[K - 1]
    z = jnp.sum(jnp.where(cand >= tau, jnp.exp(cand - top), 0.0), axis=0, keepdims=True)
    s1_ref[0] = s1
    s2_ref[0] = s2
    b_ref[0] = jnp.exp(s1 - t1[0]) / z
    c_ref[0] = jnp.exp(s2 - t2[0])
    tau_ref[0] = tau


def _peer_select(q, k1, k2):
    n = q.shape[0]
    tm = ROW_TILE
    big = jax.ShapeDtypeStruct((PK_HEADS, N_KEYS, n), F32)
    big_spec = pl.BlockSpec((1, N_KEYS, tm), lambda i, h: (h, 0, i))
    return pl.pallas_call(
        _peer_select_kernel,
        grid=(n // tm, PK_HEADS),
        in_specs=[pl.BlockSpec((tm, 2 * N_KEYS), lambda i, h: (i, h)),
                  pl.BlockSpec(k1.shape, lambda i, h: (0, 0)),
                  pl.BlockSpec(k2.shape, lambda i, h: (0, 0))],
        out_specs=[big_spec, big_spec, big_spec, big_spec, pl.BlockSpec((1, 1, tm), lambda i, h: (h, 0, i))],
        out_shape=[big, big, big, big, jax.ShapeDtypeStruct((PK_HEADS, 1, n), F32)],
        compiler_params=_cparams(("parallel", "parallel"), 8 * N_KEYS * tm * 4 + 32 * N_KEYS * tm * 4),
        name="peer_select",
    )(q, k1, k2)


def _gelu_tanh(x):
    return 0.5 * x * (1.0 + jnp.tanh(math.sqrt(2.0 / math.pi) * (x + 0.044715 * (x * x * x))))


def _peer_dense_kernel(h_ref, *refs, nj, n_split):
    u_refs, vt_refs = refs[:n_split], refs[n_split:2 * n_split]
    s1r_ref, br_ref, s2_ref, c_ref, tau_ref, x_ref, mod_ref, o_ref, acc_scr, *slot_scr = refs[2 * n_split:]
    te, d_split = u_refs[0].shape[0], u_refs[0].shape[1]

    def u_piece(ks):
        ref, off = u_refs[ks.start // d_split], (ks.start // d_split) * d_split
        return ref[:, ks.start - off:ks.stop - off].astype(BF16)

    def vt_piece(rows, ks):
        ref, off = vt_refs[rows.start // d_split], (rows.start // d_split) * d_split
        return ref[0, rows.start - off:rows.stop - off, ks]

    t = pl.program_id(0)
    tm = h_ref.shape[0]
    jc = jnp.maximum(t - 2, 0) % nj

    @pl.when(t == 0)
    def _():
        for ref in slot_scr:
            ref[...] = jnp.zeros(ref.shape, ref.dtype)

    @pl.when(jc == 0)
    def _():
        acc_scr[...] = jnp.zeros(acc_scr.shape, F32)

    def tick(s_write, s_read, w_write, w_read):
        n_slabs = te // N_KEYS
        n_key_blocks = 8
        kb = N_KEYS // n_key_blocks

        def stage_b(key_block, lc):
            keys = slice(key_block * kb, (key_block + 1) * kb)
            cols = slice(lc * LANES, (lc + 1) * LANES)
            gates = [None] * n_slabs
            for hd in range(PK_HEADS):
                s2, cf, tau = s2_ref[hd, keys, cols], c_ref[hd, keys, cols], tau_ref[hd, :, cols]
                for b in range(n_slabs):
                    total = s2 + s1r_ref[b, hd:hd + 1, cols]
                    term = jnp.where(total >= tau, cf * br_ref[b, hd:hd + 1, cols], 0.0)
                    gates[b] = term if gates[b] is None else gates[b] + term
            for b in range(n_slabs):
                rows = slice(b * N_KEYS + keys.start, b * N_KEYS + keys.stop)
                w_write[rows, cols] = (gates[b] * _gelu_tanh(s_read[rows, cols])).astype(BF16)

        nt = (((1,), (1,)), ((), ()))
        kp_size = 256
        n_a_pieces = n_split * d_split // kp_size
        n_c_pieces = te // kp_size
        n_c_blocks = 8
        c_rows = acc_scr.shape[0] // n_c_blocks
        units = [(key_block, lc) for key_block in range(n_key_blocks) for lc in range(tm // LANES)]
        n_groups = n_c_blocks * n_c_pieces
        a_per_group = 2 * n_a_pieces // n_groups
        assert a_per_group * n_groups == 2 * n_a_pieces
        a_acc, c_acc = None, None
        for g in range(n_groups):
            for i in range(a_per_group):
                half, piece = divmod(g * a_per_group + i, n_a_pieces)
                cols = slice(half * (tm // 2), (half + 1) * (tm // 2))
                ks = slice(piece * kp_size, (piece + 1) * kp_size)
                d = lax.dot_general(u_piece(ks), h_ref[cols, ks], nt, preferred_element_type=F32)
                a_acc = d if piece == 0 else a_acc + d
                if piece == n_a_pieces - 1:
                    s_write[:, cols] = a_acc
            blk, piece = divmod(g, n_c_pieces)
            rows = slice(blk * c_rows, (blk + 1) * c_rows)
            ks = slice(piece * kp_size, (piece + 1) * kp_size)
            if piece == 0:
                c_acc = acc_scr[rows, :]
            c_acc = c_acc + jnp.dot(vt_piece(rows, ks), w_read[ks, :], preferred_element_type=F32)
            if piece == n_c_pieces - 1:
                acc_scr[rows, :] = c_acc
            for _ in range(len(units) // (n_groups - g)):
                stage_b(*units.pop(0))

    parity = t % 2
    if len(slot_scr) == 2:
        s_scr, w_scr = slot_scr
        tick(s_scr.at[parity], s_scr.at[1 - parity], w_scr.at[1 - parity], w_scr.at[parity])
    else:
        s0_scr, s1_scr, w0_scr, w1_scr = slot_scr
        pl.when(parity == 0)(functools.partial(tick, s0_scr, s1_scr, w1_scr, w0_scr))
        pl.when(parity == 1)(functools.partial(tick, s1_scr, s0_scr, w0_scr, w1_scr))

    @pl.when(jnp.logical_and(jc == nj - 1, t >= 2))
    def _():
        o_ref[...] = x_ref[...] + mod_ref[0, 5:6, :] * acc_scr[...].T


def _peer_dense(h, u, vt, s1r, br, s2, c, tau, x, mod_l, *, n_rows, n_lat_rows, seq, single_body, n_split):
    D = x.shape[1]
    ds = D // n_split
    u_bytes = u.dtype.itemsize
    n_exp = u.shape[0]
    nb = mod_l.shape[0] - 1
    tm, te = ROW_TILE, PEER_TE
    nk = te // N_KEYS
    nj = n_exp // te
    n_items = (n_rows // tm) * nj
    grp = functools.partial(_group_of, tile_rows=tm, n_lat_rows=n_lat_rows, seq=seq, n_batch=nb)

    def item(t, lag):
        k = jnp.clip(t - lag, 0, n_items - 1)
        return k // nj, k % nj

    sel_spec = pl.BlockSpec((PK_HEADS, N_KEYS, tm), lambda t: (0, 0, item(t, 1)[0]))
    row_spec = pl.BlockSpec((nk, PK_HEADS, tm), lambda t: (item(t, 1)[1], 0, item(t, 1)[0]))
    vmem = (2 * tm * D * 2 + 2 * te * D * (2 + u_bytes) + 4 * PK_HEADS * N_KEYS * tm * 4 + 4 * tm * D * 4
            + D * tm * 4 + 2 * te * tm * 6 + 2 * D * tm * 4)
    u_specs = [pl.BlockSpec((te, ds), functools.partial(lambda t, k: (item(t, 0)[1], k), k=k))
               for k in range(n_split)]
    vt_specs = [pl.BlockSpec((1, ds, te), functools.partial(lambda t, k: (item(t, 2)[1], k, 0), k=k))
                for k in range(n_split)]
    return pl.pallas_call(
        functools.partial(_peer_dense_kernel, nj=nj, n_split=n_split),
        grid=(n_items + 2,),
        in_specs=[pl.BlockSpec((tm, D), lambda t: (item(t, 0)[0], 0)), *u_specs, *vt_specs,
                  row_spec, row_spec, sel_spec, sel_spec,
                  pl.BlockSpec((PK_HEADS, 1, tm), lambda t: (0, 0, item(t, 1)[0])),
                  pl.BlockSpec((tm, D), lambda t: (item(t, 2)[0], 0)),
                  pl.BlockSpec((1, N_MOD, D), lambda t: (grp(item(t, 2)[0]), 0, 0))],
        out_specs=pl.BlockSpec((tm, D), lambda t: (item(t, 2)[0], 0)),
        out_shape=jax.ShapeDtypeStruct((n_rows, D), F32),
        scratch_shapes=[pltpu.VMEM((D, tm), F32)] + (
            [pltpu.VMEM((2, te, tm), F32), pltpu.VMEM((2, te, tm), BF16)] if single_body else
            [pltpu.VMEM((te, tm), F32), pltpu.VMEM((te, tm), F32), pltpu.VMEM((te, tm), BF16),
             pltpu.VMEM((te, tm), BF16)]),
        compiler_params=_cparams(("arbitrary",), vmem),
        name="peer_dense",
    )(h, *([u] * n_split), *([vt] * n_split), s1r, br, s2, c, tau, x, mod_l)


def _rope_tables(seq, pad_rows):
    t = jnp.arange(seq, dtype=jnp.int32)
    row = (t // GRID_W).astype(F32)
    col = (t % GRID_W).astype(F32)

    def tables(dim, lanes_per_pair):
        n_pair = dim // 4
        inv = ROPE_THETA ** (-jnp.arange(n_pair, dtype=F32) / n_pair)
        ang = jnp.concatenate([row[:, None] * inv, col[:, None] * inv], axis=-1)
        lane_pair = np.arange(LANES) % lanes_per_pair
        sign = np.where(np.arange(LANES) < 64, -1.0, 1.0).astype(np.float32)
        return jnp.cos(ang)[:, lane_pair], jnp.sin(ang)[:, lane_pair] * sign

    ca, sa = tables(64, 32)
    cb, sb = tables(128, 64)
    tab = jnp.stack([ca, sa, cb, sb])
    ident = jnp.stack([jnp.ones((pad_rows, LANES), F32), jnp.zeros((pad_rows, LANES), F32)] * 2)
    return jnp.concatenate([tab, ident], axis=1)


def _permuted_proj_weight(w):
    D = w.shape[0]

    def da(blk):
        return blk.reshape(D, DA_HEADS, 2, 32, 2).transpose(0, 1, 4, 2, 3).reshape(D, DA_HEADS * LANES)

    def gq(blk):
        return blk.reshape(D, -1, 64, 2).transpose(0, 1, 3, 2).reshape(D, blk.shape[1])

    return jnp.concatenate([da(w[:, 0:512]), da(w[:, 512:1024]), gq(w[:, 1536:2048]), gq(w[:, 2048:2304]),
                            w[:, 1024:1536], w[:, 2304:5120]], axis=1)


def _qk_gains(da_qn_g, da_kn_g, gq_qn_g, gq_kn_g):
    da, gq = _da_lane_perm() % 64, _gq_lane_perm()
    log2e = math.log2(math.e)
    rows = ([da_qn_g[da] * (64 ** -0.5 * log2e)] * DA_HEADS + [da_kn_g[da]] * DA_HEADS
            + [gq_qn_g[gq] * (128 ** -0.5 * log2e)] * GQ_HEADS + [gq_kn_g[gq]] * GQ_KV)
    return jnp.stack(rows).astype(F32)


def kernel(x, c, ctx, c_ctx, mod_w, mod_b, norm1_g, norm2_g, w_in, da_qn_g, da_kn_g, da_lam, da_head_g,
           gq_qn_g, gq_kn_g, sc_w, cf_w, cf_b, cf_ln_g, cf_ln_b, w_branch, w_out,
           pk_wq, pk_k1, pk_k2, pk_u, pk_v):
    B, S, D = x.shape
    C = ctx.shape[1]
    L = mod_w.shape[0]
    n_lat, n_all = B * S, B * S + B * C
    assert S % ROW_TILE == 0 and (B * C) % ROW_TILE == 0 and C % CONV_TILE == 0 and S % min(ATT_TK, S) == 0
    assert B + 1 <= 8 and D == N_BRANCH * 512 and w_in.shape[2] == 5120 + N_BRANCH * D

    xs = jnp.concatenate([x.reshape(n_lat, D), ctx.reshape(B * C, D)], axis=0)
    cs = jnp.zeros((8, D), F32).at[:B].set(c).at[B].set(c_ctx)
    mod = _modulation(cs, mod_w, mod_b).reshape(L, 8, N_MOD, D)[:, :B + 1]
    rope_tab = _rope_tables(S, ROW_TILE)
    geom = dict(n_lat_rows=n_lat, seq=S)

    for l in range(L):
        with_ctx = l < L - 1
        n_rows = n_all if with_ctx else n_lat
        lam_init = 0.8 - 0.6 * math.exp(-0.3 * l)
        lp = da_lam[l].astype(F32)
        lam = jnp.exp(jnp.sum(lp[0] * lp[1])) - jnp.exp(jnp.sum(lp[2] * lp[3])) + lam_init
        aux = jnp.zeros((8, LANES), F32).at[0].set(lam).at[1].set(da_head_g[l] * (1.0 - lam_init))
        w_p = _permuted_proj_weight(w_in[l][:, :5120]).astype(BF16)
        w_g = w_in[l][:, 5120:].astype(BF16)
        conv_w = jnp.concatenate([sc_w[l], cf_w[l], cf_b[l][None], cf_ln_g[l][None], cf_ln_b[l][None],
                                  jnp.zeros((3, 512), F32)], axis=0)
        mod_l = mod[l]

        p = _norm_mod_matmul(xs, norm1_g[l], mod_l, w_p, shift=0, scale=1, n_rows=n_all, emit_h=False,
                             name="in_proj", **geom)
        gates = _norm_mod_matmul(xs, norm1_g[l], mod_l, w_g, shift=0, scale=1, n_rows=n_rows, emit_h=False,
                                 name="gate_proj", **geom)
        qk = _qk_prep(p, rope_tab, _qk_gains(da_qn_g[l], da_kn_g[l], gq_qn_g[l], gq_kn_g[l]), **geom)
        att = dict(n_batch=B, seq=S, ctx_len=C, with_ctx_queries=with_ctx)
        ya = _attention(qk, p, aux, diff=True, **att)
        yb = _attention(qk, p, aux, diff=False, **att)
        yc = _conv_mixers(p, conv_w, n_rows=n_rows, ctx_len=C, **geom)
        xs = _merge(ya, yb, yc, gates, xs, mod_l, w_branch[l].astype(BF16), w_out[l].astype(BF16),
                    n_rows=n_rows, **geom)

        q, h2 = _norm_mod_matmul(xs, norm2_g[l], mod_l, pk_wq[l].astype(BF16), shift=3, scale=4, n_rows=n_rows,
                                 emit_h=True, name="peer_query", **geom)
        s1, s2, bf, cf, tau = _peer_select(q, pk_k1[l].astype(BF16), pk_k2[l].astype(BF16))
        s1r, br = jnp.transpose(s1, (1, 0, 2)), jnp.transpose(bf, (1, 0, 2))
        vt = pk_v[l].reshape(-1, PEER_TE, D).transpose(0, 2, 1).astype(BF16)
        first_half = l < L // 2
        xs = _peer_dense(h2, pk_u[l] if first_half else pk_u[l].astype(BF16), vt, s1r, br, s2, cf, tau, xs, mod_l,
                         single_body=False, n_split=1 if first_half else 2, n_rows=n_rows, **geom)
    return xs[:n_lat].reshape(B, S, D)
```

```python
import functools
import math

import numpy as np
import jax
import jax.numpy as jnp
from jax import lax
from jax.experimental import pallas as pl
from jax.experimental.pallas import tpu as pltpu

F32 = jnp.float32
BF16 = jnp.bfloat16

EPS = 1e-6
N_MOD = 6
GRID_W = 64
ROPE_THETA = 10000.0
N_BRANCH = 4
DA_HEADS = 4
GQ_HEADS = 4
GQ_KV = 2
SC_K = 3
CF_K = 31
PK_HEADS = 8
N_KEYS = 128
PK_TOPK = 16

LANES = 128
V7X_VMEM_BYTES = 64 * 1024 * 1024
VMEM_CAP = V7X_VMEM_BYTES - 8 * 1024 * 1024

ROW_TILE = 512
CONV_TILE = 256
CONV_HALO = 16
ATT_TQ = 256
ATT_TK = 1024
PEER_TE = 512

QK_COLS = (2 * DA_HEADS + GQ_HEADS + GQ_KV) * LANES
V_DA_BLK = QK_COLS // LANES
V_GQ_BLK = V_DA_BLK + DA_HEADS
CONV_COL0 = (V_GQ_BLK + GQ_KV) * LANES
CONV_COLS = 5 * 512
P_COLS = CONV_COL0 + CONV_COLS


def _cparams(sem, vmem_bytes):
    limit = int(min(max(vmem_bytes * 5 // 4 + (4 << 20), 16 << 20), VMEM_CAP))
    return pltpu.CompilerParams(dimension_semantics=sem, vmem_limit_bytes=limit)


def _group_of(tile, tile_rows, n_lat_rows, seq, n_batch):
    n_lat_tiles = n_lat_rows // tile_rows
    return jnp.where(tile >= n_lat_tiles, n_batch, tile // (seq // tile_rows))


def _mod_kernel(c_ref, w_ref, b_ref, o_ref):
    c = c_ref[...]
    sc = c * (1.0 / (1.0 + jnp.exp(-c)))
    o_ref[0] = jnp.dot(sc.astype(BF16), w_ref[0].astype(BF16), preferred_element_type=F32) + b_ref[0]


def _modulation(cs, mod_w, mod_b):
    L, D, W = mod_w.shape
    tn = 1024
    return pl.pallas_call(
        _mod_kernel,
        grid=(L, W // tn),
        in_specs=[pl.BlockSpec((8, D), lambda l, j: (0, 0)),
                  pl.BlockSpec((1, D, tn), lambda l, j: (l, 0, j)),
                  pl.BlockSpec((1, 1, tn), lambda l, j: (l, 0, j))],
        out_specs=pl.BlockSpec((1, 8, tn), lambda l, j: (l, 0, j)),
        out_shape=jax.ShapeDtypeStruct((L, 8, W), F32),
        compiler_params=_cparams(("parallel", "parallel"), 2 * D * tn * 4 + D * tn * 2),
        name="modulation",
    )(cs, mod_w, mod_b.reshape(L, 1, W))


def _nmm_kernel(x_ref, g_ref, mod_ref, w_ref, *rest, shift, scale, emit_h):
    if emit_h:
        o_ref, h_ref, h_scr = rest
    else:
        o_ref, h_scr = rest

    @pl.when(pl.program_id(1) == 0)
    def _():
        x = x_ref[...]
        ms = jnp.mean(x * x, axis=-1, keepdims=True)
        y = x * lax.rsqrt(ms + EPS) * g_ref[...]
        h = y * (1.0 + mod_ref[0, scale:scale + 1, :]) + mod_ref[0, shift:shift + 1, :]
        h_scr[...] = h.astype(BF16)
        if emit_h:
            h_ref[...] = h_scr[...]

    o_ref[...] = jnp.dot(h_scr[...], w_ref[...], preferred_element_type=F32).astype(o_ref.dtype)


def _norm_mod_matmul(x, g, mod_l, w, *, shift, scale, n_rows, n_lat_rows, seq, emit_h, name):
    D = x.shape[1]
    cols = w.shape[1]
    nb = mod_l.shape[0] - 1
    tm = ROW_TILE
    tn = max(c for c in (2560, 2048, 1024, 512) if cols % c == 0)
    grp = functools.partial(_group_of, tile_rows=tm, n_lat_rows=n_lat_rows, seq=seq, n_batch=nb)
    out_shape = [jax.ShapeDtypeStruct((n_rows, cols), BF16)]
    out_specs = [pl.BlockSpec((tm, tn), lambda i, j: (i, j))]
    if emit_h:
        out_shape.append(jax.ShapeDtypeStruct((n_rows, D), BF16))
        out_specs.append(pl.BlockSpec((tm, D), lambda i, j: (i, 0)))
    vmem = 2 * tm * D * 4 + 2 * D * tn * 2 + 2 * tm * tn * 2 + 3 * tm * D * 2 + 2 * tm * D * 4
    res = pl.pallas_call(
        functools.partial(_nmm_kernel, shift=shift, scale=scale, emit_h=emit_h),
        grid=(n_rows // tm, cols // tn),
        in_specs=[pl.BlockSpec((tm, D), lambda i, j: (i, 0)),
                  pl.BlockSpec((1, D), lambda i, j: (0, 0)),
                  pl.BlockSpec((1, N_MOD, D), lambda i, j: (grp(i), 0, 0)),
                  pl.BlockSpec((D, tn), lambda i, j: (0, j))],
        out_specs=out_specs,
        out_shape=out_shape,
        scratch_shapes=[pltpu.VMEM((tm, D), BF16)],
        compiler_params=_cparams(("parallel", "arbitrary"), vmem),
        name=name,
    )(x, g.reshape(1, D), mod_l, w)
    return res if emit_h else res[0]


def _da_lane_perm():
    j = np.arange(LANES)
    comp, parity, pair = (j // 32) % 2, j // 64, j % 32
    return comp * 64 + 2 * pair + parity


def _gq_lane_perm():
    j = np.arange(LANES)
    return 2 * (j % 64) + j // 64


def _qk_prep_kernel(p_ref, tab_ref, gain_ref, o_ref):
    lane = lax.broadcasted_iota(jnp.int32, (1, LANES), 1)
    comp0 = (lane % 64) < 32
    n_da = 2 * DA_HEADS
    for hd in range(QK_COLS // LANES):
        sl = slice(hd * LANES, (hd + 1) * LANES)
        x = p_ref[:, sl].astype(F32)
        xx = x * x
        tot = jnp.sum(xx, axis=-1, keepdims=True)
        if hd < n_da:
            s0 = jnp.sum(jnp.where(comp0, xx, 0.0), axis=-1, keepdims=True)
            ms = jnp.where(comp0, s0, tot - s0) * (1.0 / 64.0)
            cos, sin = tab_ref[0], tab_ref[1]
        else:
            ms = tot * (1.0 / 128.0)
            cos, sin = tab_ref[2], tab_ref[3]
        y = x * lax.rsqrt(ms + EPS) * gain_ref[hd:hd + 1, :]
        o_ref[:, sl] = (y * cos + pltpu.roll(y, 64, 1) * sin).astype(BF16)


def _qk_prep(p, rope_tab, gains, *, n_lat_rows, seq):
    n = p.shape[0]
    tm = ROW_TILE
    n_lat_tiles = n_lat_rows // tm
    rope_blk = lambda i: jnp.where(i >= n_lat_tiles, seq // tm, i % (seq // tm))
    return pl.pallas_call(
        _qk_prep_kernel,
        grid=(n // tm,),
        in_specs=[pl.BlockSpec((tm, QK_COLS), lambda i: (i, 0)),
                  pl.BlockSpec((4, tm, LANES), lambda i: (0, rope_blk(i), 0)),
                  pl.BlockSpec((QK_COLS // LANES, LANES), lambda i: (0, 0))],
        out_specs=pl.BlockSpec((tm, QK_COLS), lambda i: (i, 0)),
        out_shape=jax.ShapeDtypeStruct((n, QK_COLS), BF16),
        compiler_params=_cparams(("parallel",), 4 * tm * QK_COLS * 2 + 8 * tm * LANES * 4 + 6 * tm * LANES * 4),
        name="qk_prep",
    )(p, rope_tab, gains)


def _flash_kernel(q_ref, kl_ref, vl_ref, kc_ref, vc_ref, aux_ref, o_ref, *, diff, n_lat_chunks, nq_lat, tk,
                  ctx_queries):
    tq = q_ref.shape[0]
    lane = lax.broadcasted_iota(jnp.int32, (1, LANES), 1)
    if diff:
        q = q_ref[...]
        comp0 = (lane % 64) < 32
        zero = jnp.zeros_like(q)
        q2 = jnp.concatenate([jnp.where(comp0, q, zero), jnp.where(comp0, zero, q)], axis=0)
    else:
        q2 = jnp.concatenate([q_ref[:, :LANES], q_ref[:, LANES:]], axis=0)

    def scores(k):
        return lax.dot_general(q2, k, (((1,), (1,)), ((), ())), preferred_element_type=F32)

    def weighted(p, v):
        ones_col = jnp.broadcast_to(jnp.where(lane == 0, 1.0, 0.0).astype(BF16), v.shape)
        return jnp.dot(p.astype(BF16), jnp.concatenate([v, ones_col], axis=1), preferred_element_type=F32)

    def fold(carry, parts):
        m_new = functools.reduce(jnp.maximum, [jnp.max(s, axis=-1, keepdims=True) for s, _ in parts])
        if carry is not None:
            m_new = jnp.maximum(carry[0], m_new)
        acc = None
        for s, v in parts:
            pv = weighted(jnp.exp2(s - m_new), v)
            acc = pv if acc is None else acc + pv
        if carry is not None:
            acc = acc + jnp.exp2(carry[0] - m_new) * carry[1]
        return m_new, acc

    def finish(carry):
        acc = carry[1]
        o = acc[:, :LANES] / acc[:, LANES:LANES + 1]
        if diff:
            d = o[:tq] - aux_ref[0:1, :] * o[tq:]
            ms = jnp.mean(d * d, axis=-1, keepdims=True)
            o_ref[...] = (d * lax.rsqrt(ms + EPS) * aux_ref[1:2, :]).astype(o_ref.dtype)
        else:
            o_ref[:, :LANES] = o[:tq].astype(o_ref.dtype)
            o_ref[:, LANES:] = o[tq:].astype(o_ref.dtype)

    def latent_queries():
        carry = None
        for c in range(n_lat_chunks):
            parts = [(scores(kl_ref[c * tk:(c + 1) * tk, :]), vl_ref[c * tk:(c + 1) * tk, :])]
            if c == n_lat_chunks - 1:
                parts.append((scores(kc_ref[...]), vc_ref[...]))
            carry = fold(carry, parts)
        finish(carry)

    if ctx_queries:
        is_lat = pl.program_id(2) < nq_lat
        pl.when(is_lat)(latent_queries)

        @pl.when(jnp.logical_not(is_lat))
        def _():
            finish(fold(None, [(scores(kc_ref[...]), vc_ref[...])]))
    else:
        latent_queries()


def _attention(qk, p, aux, *, diff, n_batch, seq, ctx_len, with_ctx_queries):
    tq, tk = ATT_TQ, min(ATT_TK, seq)
    nq_lat, nq_ctx = seq // tq, ctx_len // tq
    n_lat_rows = n_batch * seq
    if diff:
        n_heads, qw = DA_HEADS, LANES
        q_blk = lambda h: h
        k_blk = lambda h: DA_HEADS + h
        v_blk = lambda h: V_DA_BLK + h
    else:
        n_heads, qw = GQ_KV, 2 * LANES
        q_blk = lambda h: (2 * DA_HEADS * LANES) // qw + h
        k_blk = lambda h: 2 * DA_HEADS + GQ_HEADS + h
        v_blk = lambda h: V_GQ_BLK + h
    nq = nq_lat + (nq_ctx if with_ctx_queries else 0)
    q_row = lambda b, qi: jnp.where(qi >= nq_lat, n_batch * nq_lat + b * nq_ctx + qi - nq_lat, b * nq_lat + qi)
    ctx_row = lambda b: n_lat_rows // ctx_len + b
    kernel = functools.partial(_flash_kernel, diff=diff, n_lat_chunks=seq // tk, nq_lat=nq_lat, tk=tk,
                               ctx_queries=with_ctx_queries)
    n_out = n_lat_rows + (n_batch * ctx_len if with_ctx_queries else 0)
    vmem = 4 * seq * LANES * 2 + 2 * tq * (tk + ctx_len) * 4 * 4 + 8 * tq * LANES * 4 * 3
    return pl.pallas_call(
        kernel,
        grid=(n_batch, n_heads, nq),
        in_specs=[pl.BlockSpec((tq, qw), lambda b, h, qi: (q_row(b, qi), q_blk(h))),
                  pl.BlockSpec((seq, LANES), lambda b, h, qi: (b, k_blk(h))),
                  pl.BlockSpec((seq, LANES), lambda b, h, qi: (b, v_blk(h))),
                  pl.BlockSpec((ctx_len, LANES), lambda b, h, qi: (ctx_row(b), k_blk(h))),
                  pl.BlockSpec((ctx_len, LANES), lambda b, h, qi: (ctx_row(b), v_blk(h))),
                  pl.BlockSpec((8, LANES), lambda b, h, qi: (0, 0))],
        out_specs=pl.BlockSpec((tq, qw), lambda b, h, qi: (q_row(b, qi), h)),
        out_shape=jax.ShapeDtypeStruct((n_out, n_heads * qw), BF16),
        compiler_params=_cparams(("parallel", "parallel", "parallel"), vmem),
        name="diff_attention" if diff else "gqa_attention",
    )(qk, qk, p, qk, p, aux)


def _conv_kernel(cur_ref, prev_ref, next_ref, w_ref, o_ref, z_scr, u_scr, *, tiles_lat, tiles_ctx, n_lat_tiles):
    t = pl.program_id(0)
    in_lat = t < n_lat_tiles
    pos = jnp.where(in_lat, t % tiles_lat, (t - n_lat_tiles) % tiles_ctx)
    last = jnp.where(in_lat, tiles_lat, tiles_ctx) - 1
    has_prev = (pos > 0).astype(F32)
    has_next = (pos < last).astype(F32)
    T, H = cur_ref.shape[0], CONV_HALO
    W = 512

    def cols(ref, k):
        return ref[:, k * W:(k + 1) * W].astype(F32)

    def glu(ref):
        return cols(ref, 3) * (1.0 / (1.0 + jnp.exp(-cols(ref, 4))))

    z_scr[0:H, :] = cols(prev_ref, 1) * cols(prev_ref, 2) * has_prev
    z_scr[H:H + T, :] = cols(cur_ref, 1) * cols(cur_ref, 2)
    z_scr[H + T:, :] = cols(next_ref, 1) * cols(next_ref, 2) * has_next
    acc = jnp.zeros((T, W), F32)
    for k in range(SC_K):
        off = H + k - SC_K // 2
        acc = acc + w_ref[k:k + 1, :] * z_scr[off:off + T, :]
    o_ref[:, :W] = (cols(cur_ref, 0) * acc).astype(o_ref.dtype)

    u_scr[0:H, :] = glu(prev_ref) * has_prev
    u_scr[H:H + T, :] = glu(cur_ref)
    u_scr[H + T:, :] = glu(next_ref) * has_next
    acc = jnp.zeros((T, W), F32)
    for k in range(CF_K):
        off = H + k - CF_K // 2
        acc = acc + w_ref[SC_K + k:SC_K + k + 1, :] * u_scr[off:off + T, :]
    r = SC_K + CF_K
    u = acc + w_ref[r:r + 1, :]
    mu = jnp.mean(u, axis=-1, keepdims=True)
    var = jnp.mean(jnp.square(u - mu), axis=-1, keepdims=True)
    y = (u - mu) * lax.rsqrt(var + EPS) * w_ref[r + 1:r + 2, :] + w_ref[r + 2:r + 3, :]
    o_ref[:, W:] = (y * (1.0 / (1.0 + jnp.exp(-y)))).astype(o_ref.dtype)


def _conv_mixers(p, conv_w, *, n_rows, n_lat_rows, seq, ctx_len):
    T, H = CONV_TILE, CONV_HALO
    r = T // H
    n_halo_blocks = p.shape[0] // H
    cblk = CONV_COL0 // CONV_COLS
    kernel = functools.partial(_conv_kernel, tiles_lat=seq // T, tiles_ctx=ctx_len // T, n_lat_tiles=n_lat_rows // T)
    return pl.pallas_call(
        kernel,
        grid=(n_rows // T,),
        in_specs=[pl.BlockSpec((T, CONV_COLS), lambda t: (t, cblk)),
                  pl.BlockSpec((H, CONV_COLS), lambda t: (jnp.maximum(t * r - 1, 0), cblk)),
                  pl.BlockSpec((H, CONV_COLS), lambda t: (jnp.minimum((t + 1) * r, n_halo_blocks - 1), cblk)),
                  pl.BlockSpec(conv_w.shape, lambda t: (0, 0))],
        out_specs=pl.BlockSpec((T, 1024), lambda t: (t, 0)),
        out_shape=jax.ShapeDtypeStruct((n_rows, 1024), BF16),
        scratch_shapes=[pltpu.VMEM((T + 2 * H, 512), F32), pltpu.VMEM((T + 2 * H, 512), F32)],
        compiler_params=_cparams(("parallel",), 4 * T * CONV_COLS * 2 + 16 * T * 512 * 4),
        name="conv_mixers",
    )(p, p, p, conv_w)


def _merge_kernel(ya_ref, yb_ref, yc_ref, g_ref, x_ref, mod_ref, wbr_ref, wo_ref, o_ref):
    D = x_ref.shape[1]
    W = 512
    ys = (ya_ref[...], yb_ref[...], yc_ref[:, :W], yc_ref[:, W:])
    merged = jnp.zeros(x_ref.shape, F32)
    for i in range(N_BRANCH):
        z = jnp.dot(ys[i], wbr_ref[i], preferred_element_type=F32)
        gate = g_ref[:, i * D:(i + 1) * D].astype(F32)
        merged = merged + z * (1.0 / (1.0 + jnp.exp(-gate)))
    out = jnp.dot(merged.astype(BF16), wo_ref[...], preferred_element_type=F32)
    o_ref[...] = x_ref[...] + mod_ref[0, 2:3, :] * out


def _merge(ya, yb, yc, gates, x, mod_l, wbr, wo, *, n_rows, n_lat_rows, seq):
    D = x.shape[1]
    nb = mod_l.shape[0] - 1
    tm = CONV_TILE
    grp = functools.partial(_group_of, tile_rows=tm, n_lat_rows=n_lat_rows, seq=seq, n_batch=nb)
    vmem = (2 * tm * N_BRANCH * D * 2 + 4 * tm * D * 4 + N_BRANCH * 512 * D * 2 + D * D * 2
            + 4 * tm * D * 4 + 2 * tm * 2048 * 2)
    return pl.pallas_call(
        _merge_kernel,
        grid=(n_rows // tm,),
        in_specs=[pl.BlockSpec((tm, 512), lambda i: (i, 0)),
                  pl.BlockSpec((tm, 512), lambda i: (i, 0)),
                  pl.BlockSpec((tm, 1024), lambda i: (i, 0)),
                  pl.BlockSpec((tm, N_BRANCH * D), lambda i: (i, 0)),
                  pl.BlockSpec((tm, D), lambda i: (i, 0)),
                  pl.BlockSpec((1, N_MOD, D), lambda i: (grp(i), 0, 0)),
                  pl.BlockSpec(wbr.shape, lambda i: (0, 0, 0), pipeline_mode=pl.Buffered(1)),
                  pl.BlockSpec(wo.shape, lambda i: (0, 0), pipeline_mode=pl.Buffered(1))],
        out_specs=pl.BlockSpec((tm, D), lambda i: (i, 0)),
        out_shape=jax.ShapeDtypeStruct((n_rows, D), F32),
        compiler_params=_cparams(("parallel",), vmem),
        name="merge_branches",
    )(ya, yb, yc, gates, x, mod_l, wbr, wo)


def _oddeven_merge_sort_pairs(n):
    pairs = []

    def merge(lo, hi, r):
        step = r * 2
        if step < hi - lo:
            merge(lo, hi, step)
            merge(lo + r, hi, step)
            pairs.extend((i, i + r) for i in range(lo + r, hi - r, step))
        else:
            pairs.append((lo, lo + r))

    def sort(lo, hi):
        if hi - lo >= 1:
            mid = lo + (hi - lo) // 2
            sort(lo, mid)
            sort(mid + 1, hi)
            merge(lo, hi, 1)

    sort(0, n - 1)
    return pairs


def _top_values(work):
    K, SUB = PK_TOPK, 8
    T = work.shape[1]
    assert work.shape[0] % SUB == 0 and work.shape[0] <= K * SUB
    slabs = [work[SUB * i:SUB * (i + 1)] for i in range(work.shape[0] // SUB)]
    slabs += [jnp.full((SUB, T), -jnp.inf, F32)] * (K - len(slabs))

    def exchange(i, j):
        slabs[i], slabs[j] = jnp.maximum(slabs[i], slabs[j]), jnp.minimum(slabs[i], slabs[j])

    for i, j in _oddeven_merge_sort_pairs(K):
        exchange(i, j)
    for shift in (4, 2, 1):
        partner = [pltpu.roll(s, shift, 0) for s in slabs]
        slabs = [jnp.maximum(slabs[i], partner[K - 1 - i]) for i in range(K)]
        for dist in (8, 4, 2, 1):
            for i in range(K):
                if i & dist == 0:
                    exchange(i, i + dist)
    return [s[0:1] for s in slabs]


def _stack_rows(rows):
    n, T = len(rows), rows[0].shape[1]
    idx = lax.broadcasted_iota(jnp.int32, (n, T), 0)
    out = jnp.zeros((n, T), F32)
    for r, row in enumerate(rows):
        out = jnp.where(idx == r, row, out)
    return out


def _peer_select_kernel(q_ref, k1_ref, k2_ref, s1_ref, s2_ref, b_ref, c_ref, tau_ref):
    for hh in range(s1_ref.shape[0]):
        _peer_select_head(q_ref, k1_ref, k2_ref, s1_ref, s2_ref, b_ref, c_ref, tau_ref, hh)


def _peer_select_head(q_ref, k1_ref, k2_ref, s1_ref, s2_ref, b_ref, c_ref, tau_ref, hh):
    K = PK_TOPK
    nt = (((1,), (1,)), ((), ()))
    q0 = hh * 2 * N_KEYS
    s1 = lax.dot_general(k1_ref[...], q_ref[:, q0:q0 + N_KEYS], nt, preferred_element_type=F32)
    s2 = lax.dot_general(k2_ref[...], q_ref[:, q0 + N_KEYS:q0 + 2 * N_KEYS], nt, preferred_element_type=F32)
    t1 = _top_values(s1)
    t2 = _top_values(s2)
    T1, T2 = _stack_rows(t1), _stack_rows(t2)
    ninf = jnp.full((8, s1.shape[1]), -jnp.inf, F32)
    r8 = lax.broadcasted_iota(jnp.int32, (8, s1.shape[1]), 0)
    slabs = [t1[0] + T2[0:8], t1[0] + T2[8:16], t1[1] + T2[0:8]]
    slabs.append(jnp.where(r8 < K // 3, t1[2] + T2[0:8], ninf))
    slabs.append(jnp.where(r8 < K // 4, t1[3] + T2[0:8], ninf))
    slabs.append(T1[8:16] + t2[0])
    slabs.append(jnp.where(r8 >= 4, T1[0:8] + t2[0], ninf))
    slabs.append(jnp.where(r8 >= 4, T1[0:8] + t2[1], ninf))
    slabs.append(jnp.where(r8 == 4, T1[0:8] + t2[2], ninf))
    cand = jnp.concatenate(slabs, axis=0)
    best = _top_values(cand)
    top, tau = best[0], best[K - 1]
    z = jnp.sum(jnp.where(cand >= tau, jnp.exp(cand - top), 0.0), axis=0, keepdims=True)
    s1_ref[hh] = s1
    s2_ref[hh] = s2
    b_ref[hh] = jnp.exp(s1 - t1[0]) / z
    c_ref[hh] = jnp.exp(s2 - t2[0])
    tau_ref[hh] = tau


def _peer_select(q, k1, k2):
    n = q.shape[0]
    tm = ROW_TILE
    hps = 2
    big = jax.ShapeDtypeStruct((PK_HEADS, N_KEYS, n), F32)
    big_spec = pl.BlockSpec((hps, N_KEYS, tm), lambda i, h: (h, 0, i))
    return pl.pallas_call(
        _peer_select_kernel,
        grid=(n // tm, PK_HEADS // hps),
        in_specs=[pl.BlockSpec((tm, hps * 2 * N_KEYS), lambda i, h: (i, h)),
                  pl.BlockSpec(k1.shape, lambda i, h: (0, 0)),
                  pl.BlockSpec(k2.shape, lambda i, h: (0, 0))],
        out_specs=[big_spec, big_spec, big_spec, big_spec, pl.BlockSpec((hps, 1, tm), lambda i, h: (h, 0, i))],
        out_shape=[big, big, big, big, jax.ShapeDtypeStruct((PK_HEADS, 1, n), F32)],
        compiler_params=_cparams(("parallel", "parallel"), hps * (8 + 32) * N_KEYS * tm * 4),
        name="peer_select",
    )(q, k1, k2)


def _gelu_tanh(x):
    return 0.5 * x * (1.0 + jnp.tanh(math.sqrt(2.0 / math.pi) * (x + 0.044715 * (x * x * x))))


def _peer_dense_kernel(h_ref, u_ref, vt_ref, s1r_ref, br_ref, s2_ref, c_ref, tau_ref, x_ref, mod_ref, o_ref,
                       acc_scr, s0_scr, s1_scr, w0_scr, w1_scr, *, nj):
    te = u_ref.shape[0]
    t = pl.program_id(0)
    tm = h_ref.shape[0]
    jc = jnp.maximum(t - 2, 0) % nj

    @pl.when(t == 0)
    def _():
        for ref in (s0_scr, s1_scr, w0_scr, w1_scr):
            ref[...] = jnp.zeros(ref.shape, ref.dtype)

    @pl.when(jc == 0)
    def _():
        acc_scr[...] = jnp.zeros(acc_scr.shape, F32)

    def tick(s_write, s_read, w_write, w_read):
        n_slabs = te // N_KEYS
        n_key_blocks = 8
        kb = N_KEYS // n_key_blocks

        def stage_b(key_block, lc):
            keys = slice(key_block * kb, (key_block + 1) * kb)
            cols = slice(lc * LANES, (lc + 1) * LANES)
            gates = [None] * n_slabs
            for hd in range(PK_HEADS):
                s2, cf, tau = s2_ref[hd, keys, cols], c_ref[hd, keys, cols], tau_ref[hd, :, cols]
                for b in range(n_slabs):
                    total = s2 + s1r_ref[b, hd:hd + 1, cols]
                    term = jnp.where(total >= tau, cf * br_ref[b, hd:hd + 1, cols], 0.0)
                    gates[b] = term if gates[b] is None else gates[b] + term
            for b in range(n_slabs):
                rows = slice(b * N_KEYS + keys.start, b * N_KEYS + keys.stop)
                w_write[rows, cols] = (gates[b] * _gelu_tanh(s_read[rows, cols])).astype(BF16)

        nt = (((1,), (1,)), ((), ()))
        kp_size = 256
        n_a_pieces = u_ref.shape[1] // kp_size
        n_c_pieces = te // kp_size
        n_c_blocks = 8
        c_rows = acc_scr.shape[0] // n_c_blocks
        units = [(key_block, lc) for key_block in range(n_key_blocks) for lc in range(tm // LANES)]
        n_groups = n_c_blocks * n_c_pieces
        a_per_group = 2 * n_a_pieces // n_groups
        assert a_per_group * n_groups == 2 * n_a_pieces
        a_acc, c_acc = None, None
        for g in range(n_groups):
            for i in range(a_per_group):
                half, piece = divmod(g * a_per_group + i, n_a_pieces)
                cols = slice(half * (tm // 2), (half + 1) * (tm // 2))
                ks = slice(piece * kp_size, (piece + 1) * kp_size)
                d = lax.dot_general(u_ref[:, ks].astype(BF16), h_ref[cols, ks], nt, preferred_element_type=F32)
                a_acc = d if piece == 0 else a_acc + d
                if piece == n_a_pieces - 1:
                    s_write[:, cols] = a_acc
            blk, piece = divmod(g, n_c_pieces)
            rows = slice(blk * c_rows, (blk + 1) * c_rows)
            ks = slice(piece * kp_size, (piece + 1) * kp_size)
            if piece == 0:
                c_acc = acc_scr[rows, :]
            c_acc = c_acc + jnp.dot(vt_ref[0, rows, ks], w_read[ks, :], preferred_element_type=F32)
            if piece == n_c_pieces - 1:
                acc_scr[rows, :] = c_acc
            for _ in range(len(units) // (n_groups - g)):
                stage_b(*units.pop(0))

    parity = t % 2
    pl.when(parity == 0)(functools.partial(tick, s0_scr, s1_scr, w1_scr, w0_scr))
    pl.when(parity == 1)(functools.partial(tick, s1_scr, s0_scr, w0_scr, w1_scr))

    @pl.when(jnp.logical_and(jc == nj - 1, t >= 2))
    def _():
        o_ref[...] = x_ref[...] + mod_ref[0, 5:6, :] * acc_scr[...].T


def _peer_dense(h, u, vt, s1r, br, s2, c, tau, x, mod_l, *, n_rows, n_lat_rows, seq):
    D = x.shape[1]
    n_exp = u.shape[0]
    nb = mod_l.shape[0] - 1
    tm, te = ROW_TILE, PEER_TE
    nk = te // N_KEYS
    nj = n_exp // te
    n_items = (n_rows // tm) * nj
    grp = functools.partial(_group_of, tile_rows=tm, n_lat_rows=n_lat_rows, seq=seq, n_batch=nb)

    def item(t, lag):
        k = jnp.clip(t - lag, 0, n_items - 1)
        return k // nj, k % nj

    sel_spec = pl.BlockSpec((PK_HEADS, N_KEYS, tm), lambda t: (0, 0, item(t, 1)[0]))
    row_spec = pl.BlockSpec((nk, PK_HEADS, tm), lambda t: (item(t, 1)[1], 0, item(t, 1)[0]))
    vmem = (2 * tm * D * 2 + 2 * te * D * (2 + u.dtype.itemsize) + 4 * PK_HEADS * N_KEYS * tm * 4
            + 4 * tm * D * 4 + D * tm * 4 + 2 * te * tm * 6 + 2 * D * tm * 4)
    return pl.pallas_call(
        functools.partial(_peer_dense_kernel, nj=nj),
        grid=(n_items + 2,),
        in_specs=[pl.BlockSpec((tm, D), lambda t: (item(t, 0)[0], 0)),
                  pl.BlockSpec((te, D), lambda t: (item(t, 0)[1], 0)),
                  pl.BlockSpec((1, D, te), lambda t: (item(t, 2)[1], 0, 0)),
                  row_spec, row_spec, sel_spec, sel_spec,
                  pl.BlockSpec((PK_HEADS, 1, tm), lambda t: (0, 0, item(t, 1)[0])),
                  pl.BlockSpec((tm, D), lambda t: (item(t, 2)[0], 0)),
                  pl.BlockSpec((1, N_MOD, D), lambda t: (grp(item(t, 2)[0]), 0, 0))],
        out_specs=pl.BlockSpec((tm, D), lambda t: (item(t, 2)[0], 0)),
        out_shape=jax.ShapeDtypeStruct((n_rows, D), F32),
        scratch_shapes=[pltpu.VMEM((D, tm), F32), pltpu.VMEM((te, tm), F32), pltpu.VMEM((te, tm), F32),
                        pltpu.VMEM((te, tm), BF16), pltpu.VMEM((te, tm), BF16)],
        compiler_params=_cparams(("arbitrary",), vmem),
        name="peer_dense",
    )(h, u, vt, s1r, br, s2, c, tau, x, mod_l)


def _rope_tables(seq, pad_rows):
    t = jnp.arange(seq, dtype=jnp.int32)
    row = (t // GRID_W).astype(F32)
    col = (t % GRID_W).astype(F32)

    def tables(dim, lanes_per_pair):
        n_pair = dim // 4
        inv = ROPE_THETA ** (-jnp.arange(n_pair, dtype=F32) / n_pair)
        ang = jnp.concatenate([row[:, None] * inv, col[:, None] * inv], axis=-1)
        lane_pair = np.arange(LANES) % lanes_per_pair
        sign = np.where(np.arange(LANES) < 64, -1.0, 1.0).astype(np.float32)
        return jnp.cos(ang)[:, lane_pair], jnp.sin(ang)[:, lane_pair] * sign

    ca, sa = tables(64, 32)
    cb, sb = tables(128, 64)
    tab = jnp.stack([ca, sa, cb, sb])
    ident = jnp.stack([jnp.ones((pad_rows, LANES), F32), jnp.zeros((pad_rows, LANES), F32)] * 2)
    return jnp.concatenate([tab, ident], axis=1)


def _permuted_proj_weight(w):
    D = w.shape[0]

    def da(blk):
        return blk.reshape(D, DA_HEADS, 2, 32, 2).transpose(0, 1, 4, 2, 3).reshape(D, DA_HEADS * LANES)

    def gq(blk):
        return blk.reshape(D, -1, 64, 2).transpose(0, 1, 3, 2).reshape(D, blk.shape[1])

    return jnp.concatenate([da(w[:, 0:512]), da(w[:, 512:1024]), gq(w[:, 1536:2048]), gq(w[:, 2048:2304]),
                            w[:, 1024:1536], w[:, 2304:5120]], axis=1)


def _qk_gains(da_qn_g, da_kn_g, gq_qn_g, gq_kn_g):
    da, gq = _da_lane_perm() % 64, _gq_lane_perm()
    log2e = math.log2(math.e)
    rows = ([da_qn_g[da] * (64 ** -0.5 * log2e)] * DA_HEADS + [da_kn_g[da]] * DA_HEADS
            + [gq_qn_g[gq] * (128 ** -0.5 * log2e)] * GQ_HEADS + [gq_kn_g[gq]] * GQ_KV)
    return jnp.stack(rows).astype(F32)


def kernel(x, c, ctx, c_ctx, mod_w, mod_b, norm1_g, norm2_g, w_in, da_qn_g, da_kn_g, da_lam, da_head_g,
           gq_qn_g, gq_kn_g, sc_w, cf_w, cf_b, cf_ln_g, cf_ln_b, w_branch, w_out,
           pk_wq, pk_k1, pk_k2, pk_u, pk_v):
    B, S, D = x.shape
    C = ctx.shape[1]
    L = mod_w.shape[0]
    n_lat, n_all = B * S, B * S + B * C
    assert S % ROW_TILE == 0 and (B * C) % ROW_TILE == 0 and C % CONV_TILE == 0 and S % min(ATT_TK, S) == 0
    assert B + 1 <= 8 and D == N_BRANCH * 512 and w_in.shape[2] == 5120 + N_BRANCH * D

    xs = jnp.concatenate([x.reshape(n_lat, D), ctx.reshape(B * C, D)], axis=0)
    cs = jnp.zeros((8, D), F32).at[:B].set(c).at[B].set(c_ctx)
    mod = _modulation(cs, mod_w, mod_b).reshape(L, 8, N_MOD, D)[:, :B + 1]
    rope_tab = _rope_tables(S, ROW_TILE)
    geom = dict(n_lat_rows=n_lat, seq=S)

    for l in range(L):
        with_ctx = l < L - 1
        n_rows = n_all if with_ctx else n_lat
        lam_init = 0.8 - 0.6 * math.exp(-0.3 * l)
        lp = da_lam[l].astype(F32)
        lam = jnp.exp(jnp.sum(lp[0] * lp[1])) - jnp.exp(jnp.sum(lp[2] * lp[3])) + lam_init
        aux = jnp.zeros((8, LANES), F32).at[0].set(lam).at[1].set(da_head_g[l] * (1.0 - lam_init))
        w_p = _permuted_proj_weight(w_in[l][:, :5120]).astype(BF16)
        w_g = w_in[l][:, 5120:].astype(BF16)
        conv_w = jnp.concatenate([sc_w[l], cf_w[l], cf_b[l][None], cf_ln_g[l][None], cf_ln_b[l][None],
                                  jnp.zeros((3, 512), F32)], axis=0)
        mod_l = mod[l]

        p = _norm_mod_matmul(xs, norm1_g[l], mod_l, w_p, shift=0, scale=1, n_rows=n_all, emit_h=False,
                             name="in_proj", **geom)
        gates = _norm_mod_matmul(xs, norm1_g[l], mod_l, w_g, shift=0, scale=1, n_rows=n_rows, emit_h=False,
                                 name="gate_proj", **geom)
        qk = _qk_prep(p, rope_tab, _qk_gains(da_qn_g[l], da_kn_g[l], gq_qn_g[l], gq_kn_g[l]), **geom)
        att = dict(n_batch=B, seq=S, ctx_len=C, with_ctx_queries=with_ctx)
        ya = _attention(qk, p, aux, diff=True, **att)
        yb = _attention(qk, p, aux, diff=False, **att)
        yc = _conv_mixers(p, conv_w, n_rows=n_rows, ctx_len=C, **geom)
        xs = _merge(ya, yb, yc, gates, xs, mod_l, w_branch[l].astype(BF16), w_out[l].astype(BF16),
                    n_rows=n_rows, **geom)

        q, h2 = _norm_mod_matmul(xs, norm2_g[l], mod_l, pk_wq[l].astype(BF16), shift=3, scale=4, n_rows=n_rows,
                                 emit_h=True, name="peer_query", **geom)
        s1, s2, bf, cf, tau = _peer_select(q, pk_k1[l].astype(BF16), pk_k2[l].astype(BF16))
        s1r, br = jnp.transpose(s1, (1, 0, 2)), jnp.transpose(bf, (1, 0, 2))
        vt = pk_v[l].reshape(-1, PEER_TE, D).transpose(0, 2, 1).astype(BF16)
        xs = _peer_dense(h2, pk_u[l], vt, s1r, br, s2, cf, tau, xs, mod_l, n_rows=n_rows, **geom)
    return xs[:n_lat].reshape(B, S, D)
```
